```python
import jax, jax.numpy as jnp
from jax import lax
import numpy as np

D_MODEL = 1024
BATCH = 2
SEQ = 8192
DEPTH = 2
DEC_BATCH = 128
DEC_SEQ = 8
PAST_LEN = 16384
PAGE_SIZE = 128

N_AB_LAYERS = (DEPTH + 1) // 2
N_C_LAYERS = DEPTH // 2

MLA_HEADS = 8
MLA_NOPE = 64
MLA_ROPE = 32
MLA_V = 64
Q_LORA = 384
KV_LORA = 256
SB_HEADS = 8
SB_HEAD_DIM = 64
SB_WIDTH = SB_HEADS * SB_HEAD_DIM
IN_WIDTH = Q_LORA + KV_LORA + MLA_ROPE + 3 * SB_WIDTH
MIX_WIDTH = MLA_HEADS * MLA_V + SB_WIDTH
MLA_SCALE = (MLA_NOPE + MLA_ROPE) ** -0.5
SB_SCALE = SB_HEAD_DIM ** -0.5
ROPE_THETA = 10000.0
POOL_WINDOWS = (2, 4, 8, 16)
POOL_GROUP = D_MODEL // 4
POOL_HIST = 15
D_FF = -(-8 * D_MODEL // (3 * 256)) * 256
BLOCK_Q = 128
RMS_EPS = 1e-6

kernel_name = "hybrid_mla_stickbreak_pool_decoder_step"

F32 = jnp.float32


def rms_norm(x, g):
    xf = x.astype(F32)
    y = xf * lax.rsqrt(jnp.mean(xf * xf, axis=-1, keepdims=True) + RMS_EPS)
    return (y * g.astype(F32)).astype(x.dtype)


def rope(x, pos):
    half = x.shape[-1] // 2
    inv = ROPE_THETA ** (-jnp.arange(half, dtype=F32) / half)
    ang = pos[:, None] * inv[None, :]
    shape = (1, x.shape[1]) + (1,) * (x.ndim - 3) + (half,)
    cos = jnp.cos(ang).reshape(shape)
    sin = jnp.sin(ang).reshape(shape)
    xf = x.astype(F32)
    x1, x2 = xf[..., :half], xf[..., half:]
    return jnp.concatenate([x1 * cos - x2 * sin, x1 * sin + x2 * cos], axis=-1).astype(x.dtype)


def swiglu(h, wg, wu, wd):
    return (jax.nn.silu(h @ wg) * (h @ wu)) @ wd


def stick_breaking_weights(z, valid):
    log_1m = jnp.where(valid, jax.nn.log_sigmoid(-z), 0.0)
    after = lax.cumsum(log_1m, axis=z.ndim - 1, reverse=True) - log_1m
    return jnp.where(valid, jnp.exp(jax.nn.log_sigmoid(z) + after), 0.0)


def ab_project(h, pos, w_in, q_norm, w_uq, kv_norm):
    B, T, _ = h.shape
    z = h @ w_in
    o1 = Q_LORA
    o2 = o1 + KV_LORA
    o3 = o2 + MLA_ROPE
    o4 = o3 + SB_WIDTH
    o5 = o4 + SB_WIDTH
    c_q, c_kv, k_pe = z[..., :o1], z[..., o1:o2], z[..., o2:o3]
    sq, sk, sv = z[..., o3:o4], z[..., o4:o5], z[..., o5:]
    q = (rms_norm(c_q, q_norm) @ w_uq).reshape(B, T, MLA_HEADS, MLA_NOPE + MLA_ROPE)
    q_nope = q[..., :MLA_NOPE]
    q_pe = rope(q[..., MLA_NOPE:], pos)
    c_kv = rms_norm(c_kv, kv_norm)
    k_pe = rope(k_pe, pos)
    sbr = lambda t: t.reshape(B, T, SB_HEADS, SB_HEAD_DIM)
    return q_nope, q_pe, c_kv, k_pe, sbr(sq), sbr(sk), sbr(sv)


def ab_attend_prompt(q_nope, q_pe, k_nope, k_pe, v, sq, sk, sv):
    B, S = q_nope.shape[:2]
    nb = S // BLOCK_Q
    kpos = jnp.arange(S)

    def blocks(t):
        return jnp.moveaxis(t.reshape((B, nb, BLOCK_Q) + t.shape[2:]), 1, 0)

    def body(args):
        qn, qp, sqb, blk = args
        qpos = blk * BLOCK_Q + jnp.arange(BLOCK_Q)
        s = (jnp.einsum('bqhd,bkhd->bhqk', qn, k_nope, preferred_element_type=F32)
             + jnp.einsum('bqhr,bkr->bhqk', qp, k_pe, preferred_element_type=F32)) * MLA_SCALE
        p = jax.nn.softmax(jnp.where(kpos[None, :] <= qpos[:, None], s, -jnp.inf), axis=-1)
        o_mla = jnp.einsum('bhqk,bkhd->bqhd', p.astype(v.dtype), v)
        zsb = jnp.einsum('bqhd,bkhd->bhqk', sqb, sk, preferred_element_type=F32) * SB_SCALE
        a = stick_breaking_weights(zsb, kpos[None, :] < qpos[:, None])
        o_sb = jnp.einsum('bhqk,bkhd->bqhd', a.astype(sv.dtype), sv)
        return o_mla, o_sb

    o_mla, o_sb = lax.map(body, (blocks(q_nope), blocks(q_pe), blocks(sq), jnp.arange(nb)))
    unblock = lambda o: jnp.moveaxis(o, 0, 1).reshape(B, S, -1)
    return unblock(o_mla), unblock(o_sb)


def ab_mixer_prompt(h, pos, w_in, q_norm, w_uq, kv_norm, w_ukv, w_out):
    B, S, _ = h.shape
    q_nope, q_pe, c_kv, k_pe, sq, sk, sv = ab_project(h, pos, w_in, q_norm, w_uq, kv_norm)
    kv = (c_kv @ w_ukv).reshape(B, S, MLA_HEADS, MLA_NOPE + MLA_V)
    o_mla, o_sb = ab_attend_prompt(q_nope, q_pe, kv[..., :MLA_NOPE], k_pe, kv[..., MLA_NOPE:], sq, sk, sv)
    out = jnp.concatenate([o_mla, o_sb], axis=-1) @ w_out
    return out, (c_kv, k_pe, sk, sv)


def ab_mixer_sample(h, pos, cache_ckv, cache_kpe, cache_sb_k, cache_sb_v, layer, page_table,
                    w_in, q_norm, w_uq, kv_norm, w_ukv, w_out):
    B, T, _ = h.shape
    P = page_table.shape[1] * PAGE_SIZE
    q_nope, q_pe, c_kv, k_pe, sq, sk, sv = ab_project(h, pos, w_in, q_norm, w_uq, kv_norm)
    w3 = w_ukv.reshape(KV_LORA, MLA_HEADS, MLA_NOPE + MLA_V)
    q_lat = jnp.einsum('bqhn,chn->bqhc', q_nope, w3[..., :MLA_NOPE])
    kpos = jnp.arange(P + T)
    qpos = P + jnp.arange(T)
    mla_mask = kpos[None, :] <= qpos[:, None]
    sb_mask = kpos[None, :] < qpos[:, None]

    def body(args):
        pt, ql, qp, ckv_new, kpe_new, sqs, sk_new, sv_new = args
        ckv = jnp.concatenate([cache_ckv[layer, pt].reshape(P, KV_LORA), ckv_new], axis=0)
        kpe = jnp.concatenate([cache_kpe[layer, pt].reshape(P, MLA_ROPE), kpe_new], axis=0)
        s = (jnp.einsum('qhc,kc->hqk', ql, ckv, preferred_element_type=F32)
             + jnp.einsum('qhr,kr->hqk', qp, kpe, preferred_element_type=F32)) * MLA_SCALE
        p = jax.nn.softmax(jnp.where(mla_mask, s, -jnp.inf), axis=-1)
        o_lat = jnp.einsum('hqk,kc->qhc', p.astype(ckv.dtype), ckv)
        k = jnp.concatenate([cache_sb_k[layer, pt].reshape(P, SB_HEADS, SB_HEAD_DIM), sk_new], axis=0)
        v = jnp.concatenate([cache_sb_v[layer, pt].reshape(P, SB_HEADS, SB_HEAD_DIM), sv_new], axis=0)
        z = jnp.einsum('qhd,khd->hqk', sqs, k, preferred_element_type=F32) * SB_SCALE
        a = stick_breaking_weights(z, sb_mask)
        o_sb = jnp.einsum('hqk,khd->qhd', a.astype(v.dtype), v)
        return o_lat, o_sb

    o_lat, o_sb = lax.map(body, (page_table, q_lat, q_pe, c_kv, k_pe, sq, sk, sv))
    o_mla = jnp.einsum('bqhc,chv->bqhv', o_lat, w3[..., MLA_NOPE:])
    out = jnp.concatenate([o_mla.reshape(B, T, -1), o_sb.reshape(B, T, -1)], axis=-1) @ w_out
    return out, (c_kv, k_pe, sk, sv)


def pool_mix(u_ext, pos, w_pool, pool_scale):
    B, L, _ = u_ext.shape
    n = pos.shape[0]
    outs = []
    for g, w in enumerate(POOL_WINDOWS):
        ug = u_ext[..., g * POOL_GROUP:(g + 1) * POOL_GROUP].astype(F32)
        cs = jnp.cumsum(ug, axis=1)
        prev = jnp.concatenate([jnp.zeros((B, w, POOL_GROUP), F32), cs[:, :L - w]], axis=1)
        count = jnp.minimum(pos + 1.0, float(w))
        mean = (cs - prev)[:, L - n:] / count[None, :, None]
        d = (mean - ug[:, L - n:]).astype(u_ext.dtype)
        outs.append(d @ w_pool[g])
    return jnp.concatenate(outs, axis=-1) * pool_scale


def setup_inputs(seed: int = 0) -> dict:
    key = jax.random.key(seed)
    ks = jax.random.split(key, 32)
    n_pages = PAST_LEN // PAGE_SIZE
    n_phys = (DEC_BATCH * n_pages * 5) // 4
    nrm = lambda k, shape, scale: jax.random.normal(k, shape, F32) * scale
    gain = lambda k, shape: 1.0 + 0.1 * jax.random.normal(k, shape, F32)
    perm = jax.random.permutation(ks[7], n_phys)[:DEC_BATCH * n_pages]
    return {
        'x_prompt': nrm(ks[0], (BATCH, SEQ, D_MODEL), 1.0),
        'x_sample': nrm(ks[1], (DEC_BATCH, DEC_SEQ, D_MODEL), 1.0),
        'cache_ckv': nrm(ks[2], (N_AB_LAYERS, n_phys, PAGE_SIZE, KV_LORA), 1.0),
        'cache_kpe': nrm(ks[3], (N_AB_LAYERS, n_phys, PAGE_SIZE, MLA_ROPE), 1.0),
        'cache_sb_k': nrm(ks[4], (N_AB_LAYERS, n_phys, PAGE_SIZE, SB_HEADS, SB_HEAD_DIM), 1.0),
        'cache_sb_v': nrm(ks[5], (N_AB_LAYERS, n_phys, PAGE_SIZE, SB_HEADS, SB_HEAD_DIM), 1.0),
        'state_pool': nrm(ks[6], (N_C_LAYERS, DEC_BATCH, POOL_HIST, D_MODEL), 1.0),
        'page_table': perm.reshape(DEC_BATCH, n_pages).astype(jnp.int32),
        'ln_mix_pre': gain(ks[8], (DEPTH, D_MODEL)),
        'ln_mix_post': gain(ks[9], (DEPTH, D_MODEL)),
        'ln_ffn_pre': gain(ks[10], (DEPTH, D_MODEL)),
        'ln_ffn_post': gain(ks[11], (DEPTH, D_MODEL)),
        'w_in': nrm(ks[12], (N_AB_LAYERS, D_MODEL, IN_WIDTH), D_MODEL ** -0.5),
        'q_norm': gain(ks[13], (N_AB_LAYERS, Q_LORA)),
        'w_uq': nrm(ks[14], (N_AB_LAYERS, Q_LORA, MLA_HEADS * (MLA_NOPE + MLA_ROPE)), Q_LORA ** -0.5),
        'kv_norm': gain(ks[15], (N_AB_LAYERS, KV_LORA)),
        'w_ukv': nrm(ks[16], (N_AB_LAYERS, KV_LORA, MLA_HEADS * (MLA_NOPE + MLA_V)), KV_LORA ** -0.5),
        'w_out': nrm(ks[17], (N_AB_LAYERS, MIX_WIDTH, D_MODEL), MIX_WIDTH ** -0.5),
        'w_pool': nrm(ks[18], (N_C_LAYERS, 4, POOL_GROUP, POOL_GROUP), POOL_GROUP ** -0.5),
        'pool_scale': gain(ks[19], (N_C_LAYERS, D_MODEL)),
        'w_gate': nrm(ks[20], (DEPTH, D_MODEL, D_FF), D_MODEL ** -0.5),
        'w_up': nrm(ks[21], (DEPTH, D_MODEL, D_FF), D_MODEL ** -0.5),
        'w_down': nrm(ks[22], (DEPTH, D_FF, D_MODEL), D_FF ** -0.5),
    }


def reference(x_prompt, x_sample, cache_ckv, cache_kpe, cache_sb_k, cache_sb_v, state_pool, page_table,
              ln_mix_pre, ln_mix_post, ln_ffn_pre, ln_ffn_post, w_in, q_norm, w_uq, kv_norm, w_ukv, w_out,
              w_pool, pool_scale, w_gate, w_up, w_down):
    S = x_prompt.shape[1]
    T = x_sample.shape[1]
    P = page_table.shape[1] * PAGE_SIZE
    pos_p = jnp.arange(S, dtype=F32)
    pos_s = P + jnp.arange(T, dtype=F32)
    yp, ys = x_prompt, x_sample
    ab_p, ab_s, pool_p, pool_s = [], [], [], []
    for layer in range(DEPTH):
        hp = rms_norm(yp, ln_mix_pre[layer])
        hs = rms_norm(ys, ln_mix_pre[layer])
        if layer % 2 == 0:
            i = layer // 2
            wts = (w_in[i], q_norm[i], w_uq[i], kv_norm[i], w_ukv[i], w_out[i])
            mp, rows_p = ab_mixer_prompt(hp, pos_p, *wts)
            ms, rows_s = ab_mixer_sample(hs, pos_s, cache_ckv, cache_kpe, cache_sb_k, cache_sb_v, i, page_table, *wts)
            ab_p.append(rows_p)
            ab_s.append(rows_s)
        else:
            j = layer // 2
            mp = pool_mix(hp, pos_p, w_pool[j], pool_scale[j])
            u_s = jnp.concatenate([state_pool[j].astype(hs.dtype), hs], axis=1)
            ms = pool_mix(u_s, pos_s, w_pool[j], pool_scale[j])
            pool_p.append(hp[:, S - POOL_HIST:])
            pool_s.append(u_s[:, u_s.shape[1] - POOL_HIST:])
        yp = yp + rms_norm(mp, ln_mix_post[layer])
        ys = ys + rms_norm(ms, ln_mix_post[layer])
        yp = yp + rms_norm(swiglu(rms_norm(yp, ln_ffn_pre[layer]), w_gate[layer], w_up[layer], w_down[layer]), ln_ffn_post[layer])
        ys = ys + rms_norm(swiglu(rms_norm(ys, ln_ffn_pre[layer]), w_gate[layer], w_up[layer], w_down[layer]), ln_ffn_post[layer])
    new_ckv_prompt = jnp.stack([r[0] for r in ab_p])
    new_kpe_prompt = jnp.stack([r[1] for r in ab_p])
    new_sbk_prompt = jnp.stack([r[2] for r in ab_p])
    new_sbv_prompt = jnp.stack([r[3] for r in ab_p])
    new_pool_prompt = jnp.stack(pool_p)
    new_ckv_sample = jnp.stack([r[0] for r in ab_s])
    new_kpe_sample = jnp.stack([r[1] for r in ab_s])
    new_sbk_sample = jnp.stack([r[2] for r in ab_s])
    new_sbv_sample = jnp.stack([r[3] for r in ab_s])
    new_pool_sample = jnp.stack(pool_s)
    return (yp, ys, new_ckv_prompt, new_kpe_prompt, new_sbk_prompt, new_sbv_prompt, new_pool_prompt,
            new_ckv_sample, new_kpe_sample, new_sbk_sample, new_sbv_sample, new_pool_sample)
```

```python
import functools
import math

import jax
import jax.numpy as jnp
from jax import lax
from jax.experimental import pallas as pl
from jax.experimental.pallas import tpu as pltpu

F32 = jnp.float32
BF16 = jnp.bfloat16

MLA_HEADS = 8
MLA_NOPE = 64
MLA_ROPE = 32
MLA_V = 64
Q_LORA = 384
KV_LORA = 256
SB_HEADS = 8
SB_HEAD_DIM = 64
SB_WIDTH = SB_HEADS * SB_HEAD_DIM
PAGE_SIZE = 128
ROPE_THETA = 10000.0
POOL_WINDOWS = (2, 4, 8, 16)
POOL_HIST = 15
RMS_EPS = 1e-6

LOG2E = math.log2(math.e)
MLA_QSCALE = (MLA_NOPE + MLA_ROPE) ** -0.5 * LOG2E
SB_QSCALE = SB_HEAD_DIM ** -0.5 * LOG2E

LANES = 128
HALF = LANES // 2
VMEM_LIMIT = 56 * 1024 * 1024


def _cparams(sem):
    return pltpu.CompilerParams(dimension_semantics=sem, vmem_limit_bytes=VMEM_LIMIT)


def _const_spec(shape):
    nd = len(shape)
    return pl.BlockSpec(shape, lambda *_: (0,) * nd, pipeline_mode=pl.Buffered(1))


def _rms(x, g):
    ms = jnp.mean(x * x, axis=-1, keepdims=True)
    return x * lax.rsqrt(ms + RMS_EPS) * g


def _dot(a, b):
    return jnp.dot(a, b, preferred_element_type=F32)


def _dot_nt(a, b):
    return lax.dot_general(a, b, (((1,), (1,)), ((), ())), preferred_element_type=F32)


def _head_lane_mask(shape):
    lane = lax.broadcasted_iota(jnp.int32, shape, len(shape) - 1)
    return (lane % LANES) < HALF


_P_CQ, _P_CKV, _P_KPA, _P_KPB, _P_SQ, _P_SK, _P_SV, _P_END = 0, 384, 640, 768, 896, 1920, 2944, 3968
_S_SQ, _S_SK, _S_SV, _S_END = 896, 1408, 1920, 2432


def _proj_common(x_ref, g_ref, win_ref, qn_ref, kvn_ref, wa_ref, wb_ref, cos_ref, sin_ref):
    h = _rms(x_ref[...], g_ref[...]).astype(BF16)
    z = _dot(h, win_ref[...])
    cq = _rms(z[:, _P_CQ:_P_CKV], qn_ref[...]).astype(BF16)
    ckv = _rms(z[:, _P_CKV:_P_KPA], kvn_ref[...])
    cos = cos_ref[...]
    sin = sin_ref[...]
    kpe = z[:, _P_KPA:_P_KPB] * cos + z[:, _P_KPB:_P_SQ] * sin
    cos8 = jnp.tile(cos, (1, MLA_HEADS))
    sin8 = jnp.tile(sin, (1, MLA_HEADS))
    q = (_dot(cq, wa_ref[...]) * cos8 + _dot(cq, wb_ref[...]) * sin8) * MLA_QSCALE
    return z, ckv, kpe, q


def _proj_prompt_kernel(x_ref, g_ref, win_ref, qn_ref, kvn_ref, wa_ref, wb_ref, wukv_ref, cos_ref, sin_ref,
                        qm_ref, qs_ref, km_ref, ks_ref, vv_ref, ckv_ref, kpe_ref, skp_ref, svp_ref):
    z, ckv, kpe, q = _proj_common(x_ref, g_ref, win_ref, qn_ref, kvn_ref, wa_ref, wb_ref, cos_ref, sin_ref)
    skp = z[:, _P_SK:_P_SV]
    svp = z[:, _P_SV:_P_END]
    kv = _dot(ckv.astype(BF16), wukv_ref[...])
    first = _head_lane_mask(kv.shape)
    qm_ref[...] = q.astype(BF16)
    qs_ref[...] = (z[:, _P_SQ:_P_SK] * SB_QSCALE).astype(BF16)
    km_ref[...] = jnp.where(first, kv, jnp.tile(kpe, (1, MLA_HEADS))).astype(BF16)
    ks_ref[...] = skp.astype(BF16)
    vv_ref[...] = jnp.where(first, svp, kv).astype(BF16)
    ckv_ref[...] = ckv
    kpe_ref[...] = kpe
    skp_ref[...] = skp
    svp_ref[...] = svp


def _proj_sample_kernel(x_ref, g_ref, win_ref, qn_ref, kvn_ref, wa_ref, wb_ref, wabs_ref, cos_ref, sin_ref,
                        qf_ref, ql_ref, sq_ref, ckv_ref, kpe_ref, sk_ref, sv_ref):
    z, ckv, kpe, q = _proj_common(x_ref, g_ref, win_ref, qn_ref, kvn_ref, wa_ref, wb_ref, cos_ref, sin_ref)
    qf_ref[...] = q
    ql_ref[...] = _dot(q.astype(BF16), wabs_ref[...])
    sq_ref[...] = z[:, _S_SQ:_S_SK] * SB_QSCALE
    ckv_ref[...] = ckv
    kpe_ref[...] = kpe
    sk_ref[...] = z[:, _S_SK:_S_SV]
    sv_ref[...] = z[:, _S_SV:_S_END]


def _frame(w, d, off):
    k = w.shape[0]
    w3 = w.reshape(k, -1, d)
    out = jnp.zeros((k, w3.shape[1], LANES), w.dtype)
    out = out.at[:, :, off:off + d].set(w3)
    return out.reshape(k, -1)


def _swap_halves(w, d):
    k = w.shape[0]
    w3 = w.reshape(k, -1, d)
    return jnp.concatenate([w3[..., d // 2:], w3[..., :d // 2]], axis=-1).reshape(k, -1)


def _rope_tables(pos):
    half = MLA_ROPE // 2
    inv = ROPE_THETA ** (-jnp.arange(half, dtype=F32) / half)
    ang = pos[:, None] * inv[None, :]
    cos, sin = jnp.cos(ang), jnp.sin(ang)
    n = pos.shape[0]
    one = jnp.ones((n, MLA_NOPE), F32)
    zero_n = jnp.zeros((n, MLA_NOPE), F32)
    pad = jnp.zeros((n, LANES - MLA_NOPE - MLA_ROPE), F32)
    return (jnp.concatenate([one, cos, cos, pad], axis=1),
            jnp.concatenate([zero_n, -sin, sin, pad], axis=1))


def _proj_weights(w_in, w_uq, sample):
    o1 = Q_LORA
    o2 = o1 + KV_LORA
    o3 = o2 + MLA_ROPE
    o4 = o3 + SB_WIDTH
    o5 = o4 + SB_WIDTH
    w_kpe = w_in[:, o2:o3]
    cols = [w_in[:, :o2], _frame(w_kpe, MLA_ROPE, MLA_NOPE), _frame(_swap_halves(w_kpe, MLA_ROPE), MLA_ROPE, MLA_NOPE)]
    if sample:
        cols += [w_in[:, o3:]]
    else:
        cols += [_frame(w_in[:, o3:o4], SB_HEAD_DIM, 0), _frame(w_in[:, o4:o5], SB_HEAD_DIM, 0),
                 _frame(w_in[:, o5:], SB_HEAD_DIM, 0)]
    win = jnp.concatenate(cols, axis=1).astype(BF16)
    dq = MLA_NOPE + MLA_ROPE
    wq3 = w_uq.reshape(Q_LORA, MLA_HEADS, dq)
    wa = _frame(w_uq, dq, 0)
    pe_sw = _swap_halves(wq3[..., MLA_NOPE:].reshape(Q_LORA, -1), MLA_ROPE)
    wb = _frame(pe_sw, MLA_ROPE, MLA_NOPE)
    return win, wa.astype(BF16), wb.astype(BF16)


def _proj_prompt(x2, pos, g, w_in, q_norm, w_uq, kv_norm, w_ukv, tm):
    n, d = x2.shape
    win, wa, wb = _proj_weights(w_in, w_uq, sample=False)
    cos, sin = _rope_tables(pos)
    hw = MLA_HEADS * LANES
    row = lambda w: pl.BlockSpec((tm, w), lambda i: (i, 0))
    outs = [jax.ShapeDtypeStruct((n, hw), BF16)] * 5 + [
        jax.ShapeDtypeStruct((n, KV_LORA), F32), jax.ShapeDtypeStruct((n, LANES), F32),
        jax.ShapeDtypeStruct((n, hw), F32), jax.ShapeDtypeStruct((n, hw), F32)]
    return pl.pallas_call(
        _proj_prompt_kernel,
        grid=(n // tm,),
        in_specs=[row(d), _const_spec((1, d)), _const_spec(win.shape), _const_spec((1, Q_LORA)),
                  _const_spec((1, KV_LORA)), _const_spec(wa.shape), _const_spec(wb.shape),
                  _const_spec(w_ukv.shape), row(LANES), row(LANES)],
        out_specs=[row(hw)] * 5 + [row(KV_LORA), row(LANES), row(hw), row(hw)],
        out_shape=outs,
        compiler_params=_cparams(("parallel",)),
        name="proj_prompt",
    )(x2, g.reshape(1, d), win, q_norm.reshape(1, -1), kv_norm.reshape(1, -1), wa, wb,
      w_ukv.astype(BF16), cos, sin)


def _proj_sample(x2, pos, g, w_in, q_norm, w_uq, kv_norm, w_ukv, tm):
    n, d = x2.shape
    win, wa, wb = _proj_weights(w_in, w_uq, sample=True)
    cos, sin = _rope_tables(pos)
    hw = MLA_HEADS * LANES
    w3k = w_ukv.reshape(KV_LORA, MLA_HEADS, MLA_NOPE + MLA_V)[..., :MLA_NOPE]
    wabs = jnp.zeros((MLA_HEADS, LANES, MLA_HEADS, KV_LORA), F32)
    for h in range(MLA_HEADS):
        wabs = wabs.at[h, :MLA_NOPE, h, :].set(w3k[:, h, :].T)
    wabs = wabs.reshape(hw, MLA_HEADS * KV_LORA).astype(BF16)
    row = lambda w: pl.BlockSpec((tm, w), lambda i: (i, 0))
    outs = [jax.ShapeDtypeStruct((n, hw), F32), jax.ShapeDtypeStruct((n, MLA_HEADS * KV_LORA), F32),
            jax.ShapeDtypeStruct((n, SB_WIDTH), F32), jax.ShapeDtypeStruct((n, KV_LORA), F32),
            jax.ShapeDtypeStruct((n, LANES), F32), jax.ShapeDtypeStruct((n, SB_WIDTH), F32),
            jax.ShapeDtypeStruct((n, SB_WIDTH), F32)]
    return pl.pallas_call(
        _proj_sample_kernel,
        grid=(n // tm,),
        in_specs=[row(d), _const_spec((1, d)), _const_spec(win.shape), _const_spec((1, Q_LORA)),
                  _const_spec((1, KV_LORA)), _const_spec(wa.shape), _const_spec(wb.shape),
                  _const_spec(wabs.shape), row(LANES), row(LANES)],
        out_specs=[row(hw), row(MLA_HEADS * KV_LORA), row(SB_WIDTH), row(KV_LORA), row(LANES),
                   row(SB_WIDTH), row(SB_WIDTH)],
        out_shape=outs,
        compiler_params=_cparams(("parallel",)),
        name="proj_sample",
    )(x2, g.reshape(1, d), win, q_norm.reshape(1, -1), kv_norm.reshape(1, -1), wa, wb, wabs, cos, sin)


def _softmax_block(s, v, m, l, acc):
    m_new = jnp.maximum(m, jnp.max(s, axis=-1, keepdims=True))
    alpha = jnp.exp2(m - m_new)
    p = jnp.exp2(s - m_new)
    l = alpha * l + jnp.sum(p, axis=-1, keepdims=True)
    acc = alpha * acc + _dot(p.astype(BF16), v)
    return m_new, l, acc


def _stick_block(z, v, tri, c, acc, valid=None):
    e = jnp.exp2(-jnp.abs(z))
    lp = jnp.log(1.0 + e) * LOG2E
    mx = jnp.maximum(z, 0.0)
    ls = (z - mx) - lp
    nl = mx + lp
    if valid is not None:
        nl = jnp.where(valid, nl, 0.0)
    after = _dot(nl.astype(BF16), tri)
    a = jnp.exp2(ls - after - c)
    if valid is not None:
        a = jnp.where(valid, a, 0.0)
    acc = acc + _dot(a.astype(BF16), v)
    c = c + jnp.sum(nl, axis=-1, keepdims=True)
    return c, acc


def _tri(k):
    r = lax.broadcasted_iota(jnp.int32, (k, k), 0)
    s = lax.broadcasted_iota(jnp.int32, (k, k), 1)
    return jnp.where(r > s, 1.0, 0.0).astype(BF16)


def _attn_prompt_kernel(qm_ref, qs_ref, km_ref, ks_ref, vv_ref, o_ref, *, tq):
    i = pl.program_id(2)
    qm = qm_ref[...]
    qs = qs_ref[...]
    row = lax.broadcasted_iota(jnp.int32, (tq, tq), 0)
    col = lax.broadcasted_iota(jnp.int32, (tq, tq), 1)
    tri = _tri(tq)

    def blk(ref, j):
        return ref[pl.ds(pl.multiple_of(j * tq, tq), tq), :]

    s = jnp.where(col <= row, _dot_nt(qm, blk(km_ref, i)), -jnp.inf)
    m0 = jnp.full((tq, 1), -jnp.inf, F32)
    carry = _softmax_block(s, blk(vv_ref, i), m0, jnp.zeros((tq, 1), F32), jnp.zeros((tq, LANES), F32))

    def mla_body(j, carry):
        return _softmax_block(_dot_nt(qm, blk(km_ref, j)), blk(vv_ref, j), *carry)

    _, l, acc_mla = lax.fori_loop(0, i, mla_body, carry)

    c, acc_sb = _stick_block(_dot_nt(qs, blk(ks_ref, i)), blk(vv_ref, i), tri,
                             jnp.zeros((tq, 1), F32), jnp.zeros((tq, LANES), F32), valid=col < row)

    def sb_body(t, carry):
        j = i - 1 - t
        return _stick_block(_dot_nt(qs, blk(ks_ref, j)), blk(vv_ref, j), tri, *carry)

    _, acc_sb = lax.fori_loop(0, i, sb_body, (c, acc_sb))

    first = _head_lane_mask((tq, LANES))
    o_ref[...] = jnp.where(first, acc_sb, acc_mla / l).astype(o_ref.dtype)


def _attn_prompt(qm, qs, km, ks, vv, tq):
    b, s, hw = qm.shape
    nh = hw // LANES
    qspec = pl.BlockSpec((None, tq, LANES), lambda bi, h, i: (bi, i, h))
    kspec = pl.BlockSpec((None, s, LANES), lambda bi, h, i: (bi, 0, h))
    return pl.pallas_call(
        functools.partial(_attn_prompt_kernel, tq=tq),
        grid=(b, nh, s // tq),
        in_specs=[qspec, qspec, kspec, kspec, kspec],
        out_specs=qspec,
        out_shape=jax.ShapeDtypeStruct((b, s, hw), BF16),
        compiler_params=_cparams(("parallel", "parallel", "arbitrary")),
        name="attn_prompt",
    )(qm, qs, km, ks, vv)


def _attn_sample_kernel(pt_ref, qf_ref, ql_ref, sq_ref, ckvn_ref, kpen_ref, skn_ref, svn_ref,
                        ckv_ref, kpe_ref, sbk_ref, sbv_ref, olat_ref, osb_ref,
                        ql_s, qp_s, qbd_s, m_s, l_s, am_s, c_s, as_s, *, t_new):
    del pt_ref
    p = pl.program_id(1)
    nh = MLA_HEADS
    rows = nh * t_new
    page = PAGE_SIZE
    tri = _tri(page)

    @pl.when(p == 0)
    def _():
        qf = qf_ref[...]
        ql = ql_ref[...]
        sq = sq_ref[...]
        ql_s[...] = jnp.concatenate([ql[:, h * KV_LORA:(h + 1) * KV_LORA] for h in range(nh)], axis=0).astype(BF16)
        qp_s[...] = jnp.concatenate(
            [qf[:, h * LANES + MLA_NOPE:h * LANES + MLA_NOPE + MLA_ROPE] for h in range(nh)], axis=0).astype(BF16)
        lane_head = lax.broadcasted_iota(jnp.int32, (rows, SB_WIDTH), 1) // SB_HEAD_DIM
        row_head = lax.broadcasted_iota(jnp.int32, (rows, SB_WIDTH), 0) // t_new
        qbd_s[...] = jnp.where(lane_head == row_head, jnp.tile(sq, (nh, 1)), 0.0).astype(BF16)

        def pad(x):
            return jnp.concatenate([x, jnp.zeros((page - t_new, x.shape[1]), x.dtype)], axis=0).astype(BF16)

        ckv = pad(ckvn_ref[...])
        kpe = pad(kpen_ref[...])
        sbk = pad(skn_ref[...])
        sbv = pad(svn_ref[...])
        qidx = lax.broadcasted_iota(jnp.int32, (rows, page), 0) % t_new
        kidx = lax.broadcasted_iota(jnp.int32, (rows, page), 1)
        s = _dot_nt(ql_s[...], ckv) + _dot_nt(qp_s[...], kpe)
        s = jnp.where(kidx <= qidx, s, -jnp.inf)
        m, l, am = _softmax_block(s, ckv, jnp.full((rows, 1), -jnp.inf, F32), jnp.zeros((rows, 1), F32),
                                  jnp.zeros((rows, KV_LORA), F32))
        m_s[...] = m
        l_s[...] = l
        am_s[...] = am
        c, a_sb = _stick_block(_dot_nt(qbd_s[...], sbk), sbv, tri, jnp.zeros((rows, 1), F32),
                               jnp.zeros((rows, SB_WIDTH), F32), valid=kidx < qidx)
        c_s[...] = c
        as_s[...] = a_sb

    ckv = ckv_ref[...].astype(BF16)
    kpe = kpe_ref[...].astype(BF16)
    s = _dot_nt(ql_s[...], ckv) + _dot_nt(qp_s[...], kpe)
    m, l, am = _softmax_block(s, ckv, m_s[...], l_s[...], am_s[...])
    m_s[...] = m
    l_s[...] = l
    am_s[...] = am
    c, a_sb = _stick_block(_dot_nt(qbd_s[...], sbk_ref[...].astype(BF16)), sbv_ref[...].astype(BF16), tri,
                           c_s[...], as_s[...])
    c_s[...] = c
    as_s[...] = a_sb

    @pl.when(p == pl.num_programs(1) - 1)
    def _():
        olat = am_s[...] / l_s[...]
        for h in range(nh):
            olat_ref[:, h * KV_LORA:(h + 1) * KV_LORA] = olat[h * t_new:(h + 1) * t_new, :]
        a_sb = as_s[...]
        lane_head = lax.broadcasted_iota(jnp.int32, (t_new, SB_WIDTH), 1) // SB_HEAD_DIM
        o = jnp.zeros((t_new, SB_WIDTH), F32)
        for h in range(nh):
            o = o + jnp.where(lane_head == h, a_sb[h * t_new:(h + 1) * t_new, :], 0.0)
        osb_ref[...] = o


def _attn_sample(page_table, qf, ql, sq, ckv_new, kpe_new, sk_new, sv_new,
                 cache_ckv, cache_kpe, cache_sbk, cache_sbv, layer, t_new):
    nb, n_pages = page_table.shape
    rows = MLA_HEADS * t_new
    seq = lambda w: pl.BlockSpec((t_new, w), lambda b, p, pt: (b, 0))

    def paged(w):
        return pl.BlockSpec((None, None, PAGE_SIZE, w),
                            lambda b, p, pt: (layer, pt[b, n_pages - 1 - p], 0, 0))

    grid_spec = pltpu.PrefetchScalarGridSpec(
        num_scalar_prefetch=1,
        grid=(nb, n_pages),
        in_specs=[seq(MLA_HEADS * LANES), seq(MLA_HEADS * KV_LORA), seq(SB_WIDTH), seq(KV_LORA), seq(MLA_ROPE),
                  seq(SB_WIDTH), seq(SB_WIDTH), paged(KV_LORA), paged(MLA_ROPE), paged(SB_WIDTH), paged(SB_WIDTH)],
        out_specs=[seq(MLA_HEADS * KV_LORA), seq(SB_WIDTH)],
        scratch_shapes=[pltpu.VMEM((rows, KV_LORA), BF16), pltpu.VMEM((rows, MLA_ROPE), BF16),
                        pltpu.VMEM((rows, SB_WIDTH), BF16), pltpu.VMEM((rows, 1), F32), pltpu.VMEM((rows, 1), F32),
                        pltpu.VMEM((rows, KV_LORA), F32), pltpu.VMEM((rows, 1), F32),
                        pltpu.VMEM((rows, SB_WIDTH), F32)],
    )
    n = nb * t_new
    return pl.pallas_call(
        functools.partial(_attn_sample_kernel, t_new=t_new),
        grid_spec=grid_spec,
        out_shape=[jax.ShapeDtypeStruct((n, MLA_HEADS * KV_LORA), F32), jax.ShapeDtypeStruct((n, SB_WIDTH), F32)],
        compiler_params=_cparams(("parallel", "arbitrary")),
        name="attn_sample",
    )(page_table, qf, ql, sq, ckv_new, kpe_new, sk_new, sv_new, cache_ckv, cache_kpe, cache_sbk, cache_sbv)


def _ffn_tail(x, mix, g_post, g_fpre, g_fpost, wg_ref, wu_ref, wd_ref):
    y = x + _rms(mix, g_post)
    h = _rms(y, g_fpre).astype(BF16)
    gate = _dot(h, wg_ref[...])
    up = _dot(h, wu_ref[...])
    act = (gate * jax.nn.sigmoid(gate) * up).astype(BF16)
    return y + _rms(_dot(act, wd_ref[...]), g_fpost)


def _post_prompt_kernel(x_ref, a_ref, wo_ref, gp_ref, gf_ref, gq_ref, wg_ref, wu_ref, wd_ref, y_ref):
    mix = _dot(a_ref[...], wo_ref[...])
    y_ref[...] = _ffn_tail(x_ref[...], mix, gp_ref[...], gf_ref[...], gq_ref[...], wg_ref, wu_ref, wd_ref)


def _post_sample_kernel(x_ref, olat_ref, osb_ref, wv_ref, wom_ref, wos_ref, gp_ref, gf_ref, gq_ref,
                        wg_ref, wu_ref, wd_ref, y_ref):
    o_mla = _dot(olat_ref[...].astype(BF16), wv_ref[...])
    mix = _dot(o_mla.astype(BF16), wom_ref[...]) + _dot(osb_ref[...].astype(BF16), wos_ref[...])
    y_ref[...] = _ffn_tail(x_ref[...], mix, gp_ref[...], gf_ref[...], gq_ref[...], wg_ref, wu_ref, wd_ref)


def _ffn_specs(d, dff):
    return [_const_spec((1, d)), _const_spec((1, d)), _const_spec((1, d)),
            _const_spec((d, dff)), _const_spec((d, dff)), _const_spec((dff, d))]


def _post_prompt(x2, a2, w_out, g_post, g_fpre, g_fpost, wg, wu, wd, tm):
    n, d = x2.shape
    wm = w_out[:MLA_HEADS * MLA_V].reshape(MLA_HEADS, MLA_V, d)
    ws = w_out[MLA_HEADS * MLA_V:].reshape(SB_HEADS, SB_HEAD_DIM, d)
    wo = jnp.concatenate([ws, wm], axis=1).reshape(-1, d).astype(BF16)
    row = lambda w: pl.BlockSpec((tm, w), lambda i: (i, 0))
    return pl.pallas_call(
        _post_prompt_kernel,
        grid=(n // tm,),
        in_specs=[row(d), row(a2.shape[1]), _const_spec(wo.shape)] + _ffn_specs(d, wg.shape[1]),
        out_specs=row(d),
        out_shape=jax.ShapeDtypeStruct((n, d), F32),
        compiler_params=_cparams(("parallel",)),
        name="post_prompt",
    )(x2, a2, wo, g_post.reshape(1, d), g_fpre.reshape(1, d), g_fpost.reshape(1, d), wg, wu, wd)


def _post_sample(x2, olat, osb, w_ukv, w_out, g_post, g_fpre, g_fpost, wg, wu, wd, tm):
    n, d = x2.shape
    w3v = w_ukv.reshape(KV_LORA, MLA_HEADS, MLA_NOPE + MLA_V)[..., MLA_NOPE:]
    wv = jnp.zeros((MLA_HEADS, KV_LORA, MLA_HEADS, MLA_V), F32)
    for h in range(MLA_HEADS):
        wv = wv.at[h, :, h, :].set(w3v[:, h, :])
    wv = wv.reshape(MLA_HEADS * KV_LORA, MLA_HEADS * MLA_V).astype(BF16)
    wom = w_out[:MLA_HEADS * MLA_V].astype(BF16)
    wos = w_out[MLA_HEADS * MLA_V:].astype(BF16)
    row = lambda w: pl.BlockSpec((tm, w), lambda i: (i, 0))
    return pl.pallas_call(
        _post_sample_kernel,
        grid=(n // tm,),
        in_specs=[row(d), row(olat.shape[1]), row(osb.shape[1]), _const_spec(wv.shape), _const_spec(wom.shape),
                  _const_spec(wos.shape)] + _ffn_specs(d, wg.shape[1]),
        out_specs=row(d),
        out_shape=jax.ShapeDtypeStruct((n, d), F32),
        compiler_params=_cparams(("parallel",)),
        name="post_sample",
    )(x2, olat, osb, wv, wom, wos, g_post.reshape(1, d), g_fpre.reshape(1, d), g_fpost.reshape(1, d), wg, wu, wd)


HALO = 16


def _pool_mix(windows, group, wp_ref, scale):
    outs = []
    for g, w in enumerate(POOL_WINDOWS):
        cur = windows(g, 0)
        tot = cur
        for k in range(1, w):
            tot = tot + windows(g, k)
        outs.append((tot, cur))
    return outs


def _pool_prompt_kernel(x_ref, xh_ref, gm_ref, wp_ref, ps_ref, gp_ref, gf_ref, gq_ref, wg_ref, wu_ref, wd_ref,
                        y_ref, hist_ref, ext_s, *, tm, tiles_per_seq):
    i = pl.program_id(0)
    t_in_seq = i % tiles_per_seq
    x = x_ref[...]
    gm = gm_ref[...]
    hp = _rms(x, gm)
    halo = _rms(xh_ref[...], gm)
    halo = jnp.where(t_in_seq == 0, 0.0, halo)
    ext_s[0:HALO, :] = halo
    ext_s[HALO:HALO + tm, :] = hp
    hist_ref[...] = hp[tm - HALO:, :]
    grp = x.shape[1] // len(POOL_WINDOWS)
    pos = t_in_seq * tm + lax.broadcasted_iota(jnp.int32, (tm, 1), 0)
    outs = []
    for g, w in enumerate(POOL_WINDOWS):
        cs = slice(g * grp, (g + 1) * grp)
        cur = ext_s[HALO:HALO + tm, cs]
        tot = cur
        for k in range(1, w):
            tot = tot + ext_s[HALO - k:HALO - k + tm, cs]
        count = jnp.minimum(pos + 1, w).astype(F32)
        dlt = (tot / count - cur).astype(BF16)
        outs.append(_dot(dlt, wp_ref[g]))
    mix = jnp.concatenate(outs, axis=-1) * ps_ref[...]
    y_ref[...] = _ffn_tail(x, mix, gp_ref[...], gf_ref[...], gq_ref[...], wg_ref, wu_ref, wd_ref)


def _pool_prompt(x2, seq_len, g_mix, w_pool, pool_scale, g_post, g_fpre, g_fpost, wg, wu, wd, tm):
    n, d = x2.shape
    tiles_per_seq = seq_len // tm
    nseq = n // seq_len
    hb = tm // HALO
    row = pl.BlockSpec((tm, d), lambda i: (i, 0))
    halo = pl.BlockSpec((HALO, d), lambda i: (jnp.maximum(i * hb - 1, 0), 0))
    hist = pl.BlockSpec((None, HALO, d), lambda i: (i // tiles_per_seq, 0, 0))
    return pl.pallas_call(
        functools.partial(_pool_prompt_kernel, tm=tm, tiles_per_seq=tiles_per_seq),
        grid=(n // tm,),
        in_specs=[row, halo, _const_spec((1, d)), _const_spec(w_pool.shape), _const_spec((1, d))]
        + _ffn_specs(d, wg.shape[1]),
        out_specs=[row, hist],
        out_shape=[jax.ShapeDtypeStruct((n, d), F32), jax.ShapeDtypeStruct((nseq, HALO, d), F32)],
        scratch_shapes=[pltpu.VMEM((HALO + tm, d), F32)],
        compiler_params=_cparams(("arbitrary",)),
        name="pool_prompt",
    )(x2, x2, g_mix.reshape(1, d), w_pool.astype(BF16), pool_scale.reshape(1, d),
      g_post.reshape(1, d), g_fpre.reshape(1, d), g_fpost.reshape(1, d), wg, wu, wd)


def _pool_sample_kernel(x_ref, st_ref, gm_ref, wp_ref, ps_ref, gp_ref, gf_ref, gq_ref, wg_ref, wu_ref, wd_ref,
                        y_ref, hs_ref, *, t_new):
    nb, d = x_ref.shape[1], x_ref.shape[2]
    gm = gm_ref[...]
    x = x_ref[...]
    hs = _rms(x, gm)
    hs_ref[...] = hs
    ext = [st_ref[k] for k in range(POOL_HIST)] + [hs[t] for t in range(t_new)]
    grp = d // len(POOL_WINDOWS)
    outs = []
    for g, w in enumerate(POOL_WINDOWS):
        cs = slice(g * grp, (g + 1) * grp)
        dl = []
        for t in range(t_new):
            cur = ext[POOL_HIST + t][:, cs]
            tot = cur
            for k in range(1, w):
                tot = tot + ext[POOL_HIST + t - k][:, cs]
            dl.append(tot / float(w) - cur)
        dlt = jnp.concatenate(dl, axis=0).astype(BF16)
        outs.append(_dot(dlt, wp_ref[g]))
    mix = jnp.concatenate(outs, axis=-1) * ps_ref[...]
    y = _ffn_tail(x.reshape(t_new * nb, d), mix, gp_ref[...], gf_ref[...], gq_ref[...], wg_ref, wu_ref, wd_ref)
    y_ref[...] = y.reshape(t_new, nb, d)


def _pool_sample(x_t, st_t, g_mix, w_pool, pool_scale, g_post, g_fpre, g_fpost, wg, wu, wd, nb):
    t_new, b, d = x_t.shape
    slab = lambda r: pl.BlockSpec((r, nb, d), lambda i: (0, i, 0))
    return pl.pallas_call(
        functools.partial(_pool_sample_kernel, t_new=t_new),
        grid=(b // nb,),
        in_specs=[slab(t_new), slab(POOL_HIST), _const_spec((1, d)), _const_spec(w_pool.shape),
                  _const_spec((1, d))] + _ffn_specs(d, wg.shape[1]),
        out_specs=[slab(t_new), slab(t_new)],
        out_shape=[jax.ShapeDtypeStruct((t_new, b, d), F32), jax.ShapeDtypeStruct((t_new, b, d), F32)],
        compiler_params=_cparams(("parallel",)),
        name="pool_sample",
    )(x_t, st_t, g_mix.reshape(1, d), w_pool.astype(BF16), pool_scale.reshape(1, d),
      g_post.reshape(1, d), g_fpre.reshape(1, d), g_fpost.reshape(1, d), wg, wu, wd)


def kernel(x_prompt, x_sample, cache_ckv, cache_kpe, cache_sb_k, cache_sb_v, state_pool, page_table,
           ln_mix_pre, ln_mix_post, ln_ffn_pre, ln_ffn_post, w_in, q_norm, w_uq, kv_norm, w_ukv, w_out,
           w_pool, pool_scale, w_gate, w_up, w_down):
    bsz, seq, d = x_prompt.shape
    nb, t_new, _ = x_sample.shape
    n_pages = page_table.shape[1]
    past = n_pages * PAGE_SIZE
    n_p, n_s = bsz * seq, nb * t_new
    tm_p = min(512, seq)
    tm_s = min(512, n_s)
    tq = min(256, seq)

    wg = w_gate.astype(BF16)
    wu = w_up.astype(BF16)
    wd = w_down.astype(BF16)
    xp = x_prompt.reshape(n_p, d)
    xs = x_sample.reshape(n_s, d)
    pos_p = jnp.tile(jnp.arange(seq, dtype=F32), bsz)
    pos_s = jnp.tile(past + jnp.arange(t_new, dtype=F32), nb)

    lw = (ln_mix_pre[0], w_in[0], q_norm[0], w_uq[0], kv_norm[0], w_ukv[0])
    qm, qs, km, ks, vv, ckv_p, kpe_p, skp_p, svp_p = _proj_prompt(xp, pos_p, *lw, tm=min(256, seq))
    sh = (bsz, seq, MLA_HEADS * LANES)
    o_p = _attn_prompt(qm.reshape(sh), qs.reshape(sh), km.reshape(sh), ks.reshape(sh), vv.reshape(sh), tq)
    tail0 = (ln_mix_post[0], ln_ffn_pre[0], ln_ffn_post[0], wg[0], wu[0], wd[0])
    yp = _post_prompt(xp, o_p.reshape(n_p, -1), w_out[0], *tail0, tm=tm_p)

    qf, ql, sq, ckv_s, kpe_s, sk_s, sv_s = _proj_sample(xs, pos_s, *lw, tm=min(256, n_s))
    kpe_s32 = kpe_s[:, MLA_NOPE:MLA_NOPE + MLA_ROPE]
    n_phys = cache_ckv.shape[1]
    olat, osb = _attn_sample(page_table, qf, ql, sq, ckv_s, kpe_s32, sk_s, sv_s,
                             cache_ckv, cache_kpe,
                             cache_sb_k.reshape(cache_sb_k.shape[0], n_phys, PAGE_SIZE, SB_WIDTH),
                             cache_sb_v.reshape(cache_sb_v.shape[0], n_phys, PAGE_SIZE, SB_WIDTH), 0, t_new)
    ys = _post_sample(xs, olat, osb, w_ukv[0], w_out[0], *tail0, tm=tm_s)

    tail1 = (ln_mix_post[1], ln_ffn_pre[1], ln_ffn_post[1], wg[1], wu[1], wd[1])
    yp, hist_p = _pool_prompt(yp, seq, ln_mix_pre[1], w_pool[0], pool_scale[0], *tail1, tm=tm_p)
    ys_t = jnp.swapaxes(ys.reshape(nb, t_new, d), 0, 1)
    st_t = jnp.swapaxes(state_pool[0], 0, 1)
    ys_t, hs_t = _pool_sample(ys_t, st_t, ln_mix_pre[1], w_pool[0], pool_scale[0], *tail1, nb=min(64, nb))
    ys = jnp.swapaxes(ys_t, 0, 1)
    hs = jnp.swapaxes(hs_t, 0, 1)

    def unframe(x, n_rows, off):
        return x.reshape(n_rows, SB_HEADS, LANES)[:, :, off:off + SB_HEAD_DIM]

    new_pool_s = jnp.concatenate([state_pool[0], hs], axis=1)[:, t_new:]
    return (yp.reshape(bsz, seq, d), ys,
            ckv_p.reshape(1, bsz, seq, KV_LORA),
            kpe_p[:, MLA_NOPE:MLA_NOPE + MLA_ROPE].reshape(1, bsz, seq, MLA_ROPE),
            unframe(skp_p, n_p, 0).reshape(1, bsz, seq, SB_HEADS, SB_HEAD_DIM),
            unframe(svp_p, n_p, 0).reshape(1, bsz, seq, SB_HEADS, SB_HEAD_DIM),
            hist_p[:, HALO - POOL_HIST:][None],
            ckv_s.reshape(1, nb, t_new, KV_LORA),
            kpe_s32.reshape(1, nb, t_new, MLA_ROPE),
            sk_s.reshape(1, nb, t_new, SB_HEADS, SB_HEAD_DIM),
            sv_s.reshape(1, nb, t_new, SB_HEADS, SB_HEAD_DIM),
            new_pool_s[None])
```

```python
import functools
import math

import jax
import jax.numpy as jnp
from jax import lax
from jax.experimental import pallas as pl
from jax.experimental.pallas import tpu as pltpu

F32 = jnp.float32
BF16 = jnp.bfloat16

MLA_HEADS = 8
MLA_NOPE = 64
MLA_ROPE = 32
MLA_V = 64
Q_LORA = 384
KV_LORA = 256
SB_HEADS = 8
SB_HEAD_DIM = 64
SB_WIDTH = SB_HEADS * SB_HEAD_DIM
PAGE_SIZE = 128
ROPE_THETA = 10000.0
POOL_WINDOWS = (2, 4, 8, 16)
POOL_HIST = 15
RMS_EPS = 1e-6

LOG2E = math.log2(math.e)
MLA_QSCALE = (MLA_NOPE + MLA_ROPE) ** -0.5 * LOG2E
SB_QSCALE = SB_HEAD_DIM ** -0.5 * LOG2E
SB_DONE = 150.0
MLA_CHUNK_PAGES = 16

LANES = 128
HALF = LANES // 2
VMEM_LIMIT = 56 * 1024 * 1024


def _cparams(sem):
    return pltpu.CompilerParams(dimension_semantics=sem, vmem_limit_bytes=VMEM_LIMIT)


def _const_spec(shape):
    nd = len(shape)
    return pl.BlockSpec(shape, lambda *_: (0,) * nd, pipeline_mode=pl.Buffered(1))


def _rms(x, g):
    ms = jnp.mean(x * x, axis=-1, keepdims=True)
    return x * lax.rsqrt(ms + RMS_EPS) * g


def _dot(a, b):
    return jnp.dot(a, b, preferred_element_type=F32)


def _dot_nt(a, b):
    return lax.dot_general(a, b, (((1,), (1,)), ((), ())), preferred_element_type=F32)


def _head_lane_mask(shape):
    lane = lax.broadcasted_iota(jnp.int32, shape, len(shape) - 1)
    return (lane % LANES) < HALF


_P_CQ, _P_CKV, _P_KPA, _P_KPB, _P_SQ, _P_SK, _P_SV, _P_END = 0, 384, 640, 768, 896, 1920, 2944, 3968
_S_SQ, _S_SK, _S_SV, _S_END = 896, 1408, 1920, 2432


def _proj_common(x_ref, g_ref, win_ref, qn_ref, kvn_ref, wa_ref, wb_ref, cos_ref, sin_ref):
    h = _rms(x_ref[...], g_ref[...]).astype(BF16)
    z = _dot(h, win_ref[...])
    cq = _rms(z[:, _P_CQ:_P_CKV], qn_ref[...]).astype(BF16)
    ckv = _rms(z[:, _P_CKV:_P_KPA], kvn_ref[...])
    cos = cos_ref[...]
    sin = sin_ref[...]
    kpe = z[:, _P_KPA:_P_KPB] * cos + z[:, _P_KPB:_P_SQ] * sin
    cos8 = jnp.tile(cos, (1, MLA_HEADS))
    sin8 = jnp.tile(sin, (1, MLA_HEADS))
    q = (_dot(cq, wa_ref[...]) * cos8 + _dot(cq, wb_ref[...]) * sin8) * MLA_QSCALE
    return z, ckv, kpe, q


def _proj_prompt_kernel(x_ref, g_ref, win_ref, qn_ref, kvn_ref, wa_ref, wb_ref, wukv_ref, cos_ref, sin_ref,
                        qm_ref, qs_ref, km_ref, ks_ref, vv_ref, ckv_ref, kpe_ref, skp_ref, svp_ref):
    z, ckv, kpe, q = _proj_common(x_ref, g_ref, win_ref, qn_ref, kvn_ref, wa_ref, wb_ref, cos_ref, sin_ref)
    skp = z[:, _P_SK:_P_SV]
    svp = z[:, _P_SV:_P_END]
    kv = _dot(ckv.astype(BF16), wukv_ref[...])
    first = _head_lane_mask(kv.shape)
    qm_ref[...] = q.astype(BF16)
    qs_ref[...] = (z[:, _P_SQ:_P_SK] * SB_QSCALE).astype(BF16)
    km_ref[...] = jnp.where(first, kv, jnp.tile(kpe, (1, MLA_HEADS))).astype(BF16)
    ks_ref[...] = skp.astype(BF16)
    vv_ref[...] = jnp.where(first, svp, kv).astype(BF16)
    ckv_ref[...] = ckv
    kpe_ref[...] = kpe
    skp_ref[...] = skp
    svp_ref[...] = svp


def _proj_sample_kernel(x_ref, g_ref, win_ref, qn_ref, kvn_ref, wa_ref, wb_ref, wabs_ref, cos_ref, sin_ref,
                        qf_ref, ql_ref, sq_ref, ckv_ref, kpe_ref, sk_ref, sv_ref):
    z, ckv, kpe, q = _proj_common(x_ref, g_ref, win_ref, qn_ref, kvn_ref, wa_ref, wb_ref, cos_ref, sin_ref)
    qf_ref[...] = q
    ql_ref[...] = _dot(q.astype(BF16), wabs_ref[...])
    sq_ref[...] = z[:, _S_SQ:_S_SK] * SB_QSCALE
    ckv_ref[...] = ckv
    kpe_ref[...] = kpe
    sk_ref[...] = z[:, _S_SK:_S_SV]
    sv_ref[...] = z[:, _S_SV:_S_END]


def _frame(w, d, off):
    k = w.shape[0]
    w3 = w.reshape(k, -1, d)
    out = jnp.zeros((k, w3.shape[1], LANES), w.dtype)
    out = out.at[:, :, off:off + d].set(w3)
    return out.reshape(k, -1)


def _swap_halves(w, d):
    k = w.shape[0]
    w3 = w.reshape(k, -1, d)
    return jnp.concatenate([w3[..., d // 2:], w3[..., :d // 2]], axis=-1).reshape(k, -1)


def _rope_tables(pos):
    half = MLA_ROPE // 2
    inv = ROPE_THETA ** (-jnp.arange(half, dtype=F32) / half)
    ang = pos[:, None] * inv[None, :]
    cos, sin = jnp.cos(ang), jnp.sin(ang)
    n = pos.shape[0]
    one = jnp.ones((n, MLA_NOPE), F32)
    zero_n = jnp.zeros((n, MLA_NOPE), F32)
    pad = jnp.zeros((n, LANES - MLA_NOPE - MLA_ROPE), F32)
    return (jnp.concatenate([one, cos, cos, pad], axis=1),
            jnp.concatenate([zero_n, -sin, sin, pad], axis=1))


def _proj_weights(w_in, w_uq, sample):
    o1 = Q_LORA
    o2 = o1 + KV_LORA
    o3 = o2 + MLA_ROPE
    o4 = o3 + SB_WIDTH
    o5 = o4 + SB_WIDTH
    w_kpe = w_in[:, o2:o3]
    cols = [w_in[:, :o2], _frame(w_kpe, MLA_ROPE, MLA_NOPE), _frame(_swap_halves(w_kpe, MLA_ROPE), MLA_ROPE, MLA_NOPE)]
    if sample:
        cols += [w_in[:, o3:]]
    else:
        cols += [_frame(w_in[:, o3:o4], SB_HEAD_DIM, 0), _frame(w_in[:, o4:o5], SB_HEAD_DIM, 0),
                 _frame(w_in[:, o5:], SB_HEAD_DIM, 0)]
    win = jnp.concatenate(cols, axis=1).astype(BF16)
    dq = MLA_NOPE + MLA_ROPE
    wq3 = w_uq.reshape(Q_LORA, MLA_HEADS, dq)
    wa = _frame(w_uq, dq, 0)
    pe_sw = _swap_halves(wq3[..., MLA_NOPE:].reshape(Q_LORA, -1), MLA_ROPE)
    wb = _frame(pe_sw, MLA_ROPE, MLA_NOPE)
    return win, wa.astype(BF16), wb.astype(BF16)


def _proj_prompt(x2, pos, g, w_in, q_norm, w_uq, kv_norm, w_ukv, tm):
    n, d = x2.shape
    win, wa, wb = _proj_weights(w_in, w_uq, sample=False)
    cos, sin = _rope_tables(pos)
    hw = MLA_HEADS * LANES
    row = lambda w: pl.BlockSpec((tm, w), lambda i: (i, 0))
    outs = [jax.ShapeDtypeStruct((n, hw), BF16)] * 5 + [
        jax.ShapeDtypeStruct((n, KV_LORA), F32), jax.ShapeDtypeStruct((n, LANES), F32),
        jax.ShapeDtypeStruct((n, hw), F32), jax.ShapeDtypeStruct((n, hw), F32)]
    return pl.pallas_call(
        _proj_prompt_kernel,
        grid=(n // tm,),
        in_specs=[row(d), _const_spec((1, d)), _const_spec(win.shape), _const_spec((1, Q_LORA)),
                  _const_spec((1, KV_LORA)), _const_spec(wa.shape), _const_spec(wb.shape),
                  _const_spec(w_ukv.shape), row(LANES), row(LANES)],
        out_specs=[row(hw)] * 5 + [row(KV_LORA), row(LANES), row(hw), row(hw)],
        out_shape=outs,
        compiler_params=_cparams(("parallel",)),
        name="proj_prompt",
    )(x2, g.reshape(1, d), win, q_norm.reshape(1, -1), kv_norm.reshape(1, -1), wa, wb,
      w_ukv.astype(BF16), cos, sin)


def _proj_sample(x2, pos, g, w_in, q_norm, w_uq, kv_norm, w_ukv, tm):
    n, d = x2.shape
    win, wa, wb = _proj_weights(w_in, w_uq, sample=True)
    cos, sin = _rope_tables(pos)
    hw = MLA_HEADS * LANES
    w3k = w_ukv.reshape(KV_LORA, MLA_HEADS, MLA_NOPE + MLA_V)[..., :MLA_NOPE]
    wabs = jnp.zeros((MLA_HEADS, LANES, MLA_HEADS, KV_LORA), F32)
    for h in range(MLA_HEADS):
        wabs = wabs.at[h, :MLA_NOPE, h, :].set(w3k[:, h, :].T)
    wabs = wabs.reshape(hw, MLA_HEADS * KV_LORA).astype(BF16)
    row = lambda w: pl.BlockSpec((tm, w), lambda i: (i, 0))
    outs = [jax.ShapeDtypeStruct((n, hw), F32), jax.ShapeDtypeStruct((n, MLA_HEADS * KV_LORA), F32),
            jax.ShapeDtypeStruct((n, SB_WIDTH), F32), jax.ShapeDtypeStruct((n, KV_LORA), F32),
            jax.ShapeDtypeStruct((n, LANES), F32), jax.ShapeDtypeStruct((n, SB_WIDTH), F32),
            jax.ShapeDtypeStruct((n, SB_WIDTH), F32)]
    return pl.pallas_call(
        _proj_sample_kernel,
        grid=(n // tm,),
        in_specs=[row(d), _const_spec((1, d)), _const_spec(win.shape), _const_spec((1, Q_LORA)),
                  _const_spec((1, KV_LORA)), _const_spec(wa.shape), _const_spec(wb.shape),
                  _const_spec(wabs.shape), row(LANES), row(LANES)],
        out_specs=[row(hw), row(MLA_HEADS * KV_LORA), row(SB_WIDTH), row(KV_LORA), row(LANES),
                   row(SB_WIDTH), row(SB_WIDTH)],
        out_shape=outs,
        compiler_params=_cparams(("parallel",)),
        name="proj_sample",
    )(x2, g.reshape(1, d), win, q_norm.reshape(1, -1), kv_norm.reshape(1, -1), wa, wb, wabs, cos, sin)


def _softmax_block(s, v, m, l, acc):
    m_new = jnp.maximum(m, jnp.max(s, axis=-1, keepdims=True))
    alpha = jnp.exp2(m - m_new)
    p = jnp.exp2(s - m_new)
    l = alpha * l + jnp.sum(p, axis=-1, keepdims=True)
    acc = alpha * acc + _dot(p.astype(BF16), v)
    return m_new, l, acc


def _stick_block(z, v, tri, c, acc, valid=None, v_transposed=False):
    e = jnp.exp2(-jnp.abs(z))
    lp = jnp.log(1.0 + e) * LOG2E
    mx = jnp.maximum(z, 0.0)
    ls = (z - mx) - lp
    nl = mx + lp
    if valid is not None:
        nl = jnp.where(valid, nl, 0.0)
    after = _dot(nl.astype(BF16), tri)
    a = jnp.exp2(ls - after - c)
    if valid is not None:
        a = jnp.where(valid, a, 0.0)
    a = a.astype(BF16)
    acc = acc + (_dot_nt(a, v) if v_transposed else _dot(a, v))
    c = c + jnp.sum(nl, axis=-1, keepdims=True)
    return c, acc


def _tri(k):
    r = lax.broadcasted_iota(jnp.int32, (k, k), 0)
    s = lax.broadcasted_iota(jnp.int32, (k, k), 1)
    return jnp.where(r > s, 1.0, 0.0).astype(BF16)


def _all_done(cs):
    cmin = cs[0]
    for c in cs[1:]:
        cmin = jnp.minimum(cmin, c)
    return (jnp.min(cmin) > SB_DONE).astype(jnp.int32)


def _attn_prompt_kernel(qm_ref, qs_ref, km_ref, ks_ref, vv_ref, o_ref, *, tq, hg):
    i = pl.program_id(2)
    heads = range(hg)
    hs = [pl.ds(h * LANES, LANES) for h in heads]
    qm = [qm_ref[:, hs[h]] for h in heads]
    qs = [qs_ref[:, hs[h]] for h in heads]
    tk2 = 2 * tq

    def rows(start, size):
        return pl.ds(pl.multiple_of(start, tq), size)

    base = (i // 2) * tk2
    qpos = i * tq + lax.broadcasted_iota(jnp.int32, (tq, tk2), 0)
    kpos = base + lax.broadcasted_iota(jnp.int32, (tq, tk2), 1)
    causal = kpos <= qpos
    m0 = jnp.full((tq, 1), -jnp.inf, F32)
    state = []
    for h in heads:
        s = jnp.where(causal, _dot_nt(qm[h], km_ref[rows(base, tk2), hs[h]]), -jnp.inf)
        state.append(_softmax_block(s, vv_ref[rows(base, tk2), hs[h]], m0,
                                    jnp.zeros((tq, 1), F32), jnp.zeros((tq, LANES), F32)))

    def mla_body(p, state):
        r = rows(p * tk2, tk2)
        return tuple(_softmax_block(_dot_nt(qm[h], km_ref[r, hs[h]]), vv_ref[r, hs[h]], *state[h])
                     for h in heads)

    state = lax.fori_loop(0, i // 2, mla_body, tuple(state))

    row = lax.broadcasted_iota(jnp.int32, (tq, tq), 0)
    col = lax.broadcasted_iota(jnp.int32, (tq, tq), 1)
    tri = _tri(tq)
    diag = rows(i * tq, tq)
    sb = tuple(_stick_block(_dot_nt(qs[h], ks_ref[diag, hs[h]]), vv_ref[diag, hs[h]], tri,
                            jnp.zeros((tq, 1), F32), jnp.zeros((tq, LANES), F32), valid=col < row)
               for h in heads)

    def sb_cond(carry):
        j, done, _ = carry
        return jnp.logical_and(j >= 0, done == 0)

    def sb_body(carry):
        j, _, sb = carry
        r = rows(j * tq, tq)
        sb = tuple(_stick_block(_dot_nt(qs[h], ks_ref[r, hs[h]]), vv_ref[r, hs[h]], tri, *sb[h])
                   for h in heads)
        return j - 1, _all_done([st[0] for st in sb]), sb

    _, _, sb = lax.while_loop(sb_cond, sb_body, (i - 1, jnp.int32(0), sb))

    first = _head_lane_mask((tq, LANES))
    for h in heads:
        _, l, acc_mla = state[h]
        o_ref[:, hs[h]] = jnp.where(first, sb[h][1], acc_mla / l).astype(o_ref.dtype)


def _attn_prompt(qm, qs, km, ks, vv, tq, hg):
    b, s, hw = qm.shape
    nh = hw // LANES
    assert nh % hg == 0 and s % (2 * tq) == 0
    qspec = pl.BlockSpec((None, tq, hg * LANES), lambda bi, h, i: (bi, i, h))
    kspec = pl.BlockSpec((None, s, hg * LANES), lambda bi, h, i: (bi, 0, h))
    return pl.pallas_call(
        functools.partial(_attn_prompt_kernel, tq=tq, hg=hg),
        grid=(b, nh // hg, s // tq),
        in_specs=[qspec, qspec, kspec, kspec, kspec],
        out_specs=qspec,
        out_shape=jax.ShapeDtypeStruct((b, s, hw), BF16),
        compiler_params=_cparams(("parallel", "parallel", "arbitrary")),
        name="attn_prompt",
    )(qm, qs, km, ks, vv)


def _attn_sample_kernel(pt_ref, qf_ref, ql_ref, sq_ref, ckvn_ref, kpen_ref, skn_ref, svn_ref,
                        ckv_hbm, kpe_hbm, sbk_hbm, sbv_hbm, olat_ref, osb_ref,
                        ckv_buf, kpe_buf, sbk_buf, sbv_buf, ckv_sem, kpe_sem, sbk_sem, sbv_sem,
                        *, t_new, layer, n_pages, chunk_pages):
    b = pl.program_id(0)
    nb = pl.num_programs(0)
    nh = MLA_HEADS
    rows = nh * t_new
    page = PAGE_SIZE
    n_chunks = n_pages // chunk_pages
    n_pairs = n_chunks // 2

    def ckv_copy(pg, slot, p):
        return pltpu.make_async_copy(ckv_hbm.at[layer, pg], ckv_buf.at[slot, pl.ds(p * page, page), :],
                                     ckv_sem.at[slot])

    def kpe_copy(pg, slot, p):
        return pltpu.make_async_copy(kpe_hbm.at[layer, pg], kpe_buf.at[slot, :, pl.ds(p * page, page)],
                                     kpe_sem.at[slot])

    def sbk_copy(pg, slot):
        return pltpu.make_async_copy(sbk_hbm.at[layer, pg], sbk_buf.at[slot], sbk_sem.at[slot])

    def sbv_copy(pg, slot):
        return pltpu.make_async_copy(sbv_hbm.at[layer, pg], sbv_buf.at[slot], sbv_sem.at[slot])

    def start_chunk(seq, c, slot):
        for p in range(chunk_pages):
            pg = pt_ref[seq, c * chunk_pages + p]
            ckv_copy(pg, slot, p).start()
            kpe_copy(pg, slot, p).start()

    def wait_chunk(slot):
        for p in range(chunk_pages):
            ckv_copy(0, slot, p).wait()
            kpe_copy(0, slot, p).wait()

    def start_sb(pg, slot):
        sbk_copy(pg, slot).start()
        sbv_copy(pg, slot).start()

    def wait_sb(slot):
        sbk_copy(0, slot).wait()
        sbv_copy(0, slot).wait()

    @pl.when(b == 0)
    def _():
        start_chunk(0, 0, 0)

    start_sb(pt_ref[b, n_pages - 1], 0)
    start_sb(pt_ref[b, n_pages - 2], 1)

    qf = qf_ref[...]
    qlr = ql_ref[...]
    sq = sq_ref[...]
    ql = jnp.concatenate([qlr[:, h * KV_LORA:(h + 1) * KV_LORA] for h in range(nh)], axis=0).astype(BF16)
    qp = jnp.concatenate(
        [qf[:, h * LANES + MLA_NOPE:h * LANES + MLA_NOPE + MLA_ROPE] for h in range(nh)], axis=0).astype(BF16)
    lane_head = lax.broadcasted_iota(jnp.int32, (rows, SB_WIDTH), 1) // SB_HEAD_DIM
    row_head = lax.broadcasted_iota(jnp.int32, (rows, SB_WIDTH), 0) // t_new
    qbd = jnp.where(lane_head == row_head, jnp.tile(sq, (nh, 1)), 0.0).astype(BF16)

    def pad(x):
        return jnp.concatenate([x, jnp.zeros((page - t_new, x.shape[1]), x.dtype)], axis=0).astype(BF16)

    qidx = lax.broadcasted_iota(jnp.int32, (rows, page), 0) % t_new
    kidx = lax.broadcasted_iota(jnp.int32, (rows, page), 1)
    ckv_n = pad(ckvn_ref[...])
    s = _dot_nt(ql, ckv_n) + _dot_nt(qp, pad(kpen_ref[...]))
    s = jnp.where(kidx <= qidx, s, -jnp.inf)
    mla = _softmax_block(s, ckv_n, jnp.full((rows, 1), -jnp.inf, F32), jnp.zeros((rows, 1), F32),
                         jnp.zeros((rows, KV_LORA), F32))

    def mla_chunk(slot, mla):
        ckv = ckv_buf[slot].astype(BF16)
        s = _dot_nt(ql, ckv) + _dot(qp, kpe_buf[slot].astype(BF16))
        return _softmax_block(s, ckv, *mla)

    def pair_body(t, mla):
        c0 = 2 * t
        start_chunk(b, c0 + 1, 1)
        wait_chunk(0)
        mla = mla_chunk(0, mla)

        @pl.when(c0 + 2 < n_chunks)
        def _():
            start_chunk(b, c0 + 2, 0)

        @pl.when(jnp.logical_and(c0 + 2 >= n_chunks, b + 1 < nb))
        def _():
            start_chunk(b + 1, 0, 0)

        wait_chunk(1)
        return mla_chunk(1, mla)

    _, l, acc = lax.fori_loop(0, n_pairs, pair_body, mla)
    olat = acc / l
    for h in range(nh):
        olat_ref[:, h * KV_LORA:(h + 1) * KV_LORA] = olat[h * t_new:(h + 1) * t_new, :]

    tri = _tri(page)
    sb = _stick_block(_dot_nt(qbd, pad(skn_ref[...])), pad(svn_ref[...]), tri, jnp.zeros((rows, 1), F32),
                      jnp.zeros((rows, SB_WIDTH), F32), valid=kidx < qidx)

    def sb_page(slot, sb):
        return _stick_block(_dot(qbd, sbk_buf[slot].astype(BF16)), sbv_buf[slot].astype(BF16), tri, *sb,
                            v_transposed=True)

    wait_sb(0)
    sb = sb_page(0, sb)
    wait_sb(1)
    sb = sb_page(1, sb)

    def sb_cond(carry):
        j, done, _ = carry
        return jnp.logical_and(j >= 0, done == 0)

    def sb_body(carry):
        j, _, sb = carry
        start_sb(pt_ref[b, j], 0)
        wait_sb(0)
        sb = sb_page(0, sb)
        return j - 1, _all_done([sb[0]]), sb

    _, _, sb = lax.while_loop(sb_cond, sb_body, (jnp.int32(n_pages - 3), _all_done([sb[0]]), sb))
    a_sb = sb[1]
    head_of_lane = lax.broadcasted_iota(jnp.int32, (t_new, SB_WIDTH), 1) // SB_HEAD_DIM
    o = jnp.zeros((t_new, SB_WIDTH), F32)
    for h in range(nh):
        o = o + jnp.where(head_of_lane == h, a_sb[h * t_new:(h + 1) * t_new, :], 0.0)
    osb_ref[...] = o


def _attn_sample(page_table, qf, ql, sq, ckv_new, kpe_new, sk_new, sv_new,
                 cache_ckv, cache_kpe_t, cache_sbk_t, cache_sbv_t, layer, t_new):
    nb, n_pages = page_table.shape
    chunk_pages = min(MLA_CHUNK_PAGES, n_pages // 2)
    assert n_pages >= 2 and n_pages % (2 * chunk_pages) == 0
    seq = lambda w: pl.BlockSpec((t_new, w), lambda b, pt: (b, 0))
    hbm = pl.BlockSpec(memory_space=pl.ANY)
    ck = chunk_pages * PAGE_SIZE
    grid_spec = pltpu.PrefetchScalarGridSpec(
        num_scalar_prefetch=1,
        grid=(nb,),
        in_specs=[seq(MLA_HEADS * LANES), seq(MLA_HEADS * KV_LORA), seq(SB_WIDTH), seq(KV_LORA), seq(MLA_ROPE),
                  seq(SB_WIDTH), seq(SB_WIDTH), hbm, hbm, hbm, hbm],
        out_specs=[seq(MLA_HEADS * KV_LORA), seq(SB_WIDTH)],
        scratch_shapes=[pltpu.VMEM((2, ck, KV_LORA), F32), pltpu.VMEM((2, MLA_ROPE, ck), F32),
                        pltpu.VMEM((2, SB_WIDTH, PAGE_SIZE), F32), pltpu.VMEM((2, SB_WIDTH, PAGE_SIZE), F32),
                        pltpu.SemaphoreType.DMA((2,)), pltpu.SemaphoreType.DMA((2,)),
                        pltpu.SemaphoreType.DMA((2,)), pltpu.SemaphoreType.DMA((2,))],
    )
    n = nb * t_new
    return pl.pallas_call(
        functools.partial(_attn_sample_kernel, t_new=t_new, layer=layer, n_pages=n_pages,
                          chunk_pages=chunk_pages),
        grid_spec=grid_spec,
        out_shape=[jax.ShapeDtypeStruct((n, MLA_HEADS * KV_LORA), F32), jax.ShapeDtypeStruct((n, SB_WIDTH), F32)],
        compiler_params=_cparams(("arbitrary",)),
        name="attn_sample",
    )(page_table, qf, ql, sq, ckv_new, kpe_new, sk_new, sv_new, cache_ckv, cache_kpe_t, cache_sbk_t, cache_sbv_t)


def _ffn_tail(x, mix, g_post, g_fpre, g_fpost, wg_ref, wu_ref, wd_ref):
    y = x + _rms(mix, g_post)
    h = _rms(y, g_fpre).astype(BF16)
    gate = _dot(h, wg_ref[...])
    up = _dot(h, wu_ref[...])
    act = (gate * jax.nn.sigmoid(gate) * up).astype(BF16)
    return y + _rms(_dot(act, wd_ref[...]), g_fpost)


def _post_prompt_kernel(x_ref, a_ref, wo_ref, gp_ref, gf_ref, gq_ref, wg_ref, wu_ref, wd_ref, y_ref):
    mix = _dot(a_ref[...], wo_ref[...])
    y_ref[...] = _ffn_tail(x_ref[...], mix, gp_ref[...], gf_ref[...], gq_ref[...], wg_ref, wu_ref, wd_ref)


def _post_sample_kernel(x_ref, olat_ref, osb_ref, wv_ref, wom_ref, wos_ref, gp_ref, gf_ref, gq_ref,
                        wg_ref, wu_ref, wd_ref, y_ref):
    o_mla = _dot(olat_ref[...].astype(BF16), wv_ref[...])
    mix = _dot(o_mla.astype(BF16), wom_ref[...]) + _dot(osb_ref[...].astype(BF16), wos_ref[...])
    y_ref[...] = _ffn_tail(x_ref[...], mix, gp_ref[...], gf_ref[...], gq_ref[...], wg_ref, wu_ref, wd_ref)


def _ffn_specs(d, dff):
    return [_const_spec((1, d)), _const_spec((1, d)), _const_spec((1, d)),
            _const_spec((d, dff)), _const_spec((d, dff)), _const_spec((dff, d))]


def _post_prompt(x2, a2, w_out, g_post, g_fpre, g_fpost, wg, wu, wd, tm):
    n, d = x2.shape
    wm = w_out[:MLA_HEADS * MLA_V].reshape(MLA_HEADS, MLA_V, d)
    ws = w_out[MLA_HEADS * MLA_V:].reshape(SB_HEADS, SB_HEAD_DIM, d)
    wo = jnp.concatenate([ws, wm], axis=1).reshape(-1, d).astype(BF16)
    row = lambda w: pl.BlockSpec((tm, w), lambda i: (i, 0))
    return pl.pallas_call(
        _post_prompt_kernel,
        grid=(n // tm,),
        in_specs=[row(d), row(a2.shape[1]), _const_spec(wo.shape)] + _ffn_specs(d, wg.shape[1]),
        out_specs=row(d),
        out_shape=jax.ShapeDtypeStruct((n, d), F32),
        compiler_params=_cparams(("parallel",)),
        name="post_prompt",
    )(x2, a2, wo, g_post.reshape(1, d), g_fpre.reshape(1, d), g_fpost.reshape(1, d), wg, wu, wd)


def _post_sample(x2, olat, osb, w_ukv, w_out, g_post, g_fpre, g_fpost, wg, wu, wd, tm):
    n, d = x2.shape
    w3v = w_ukv.reshape(KV_LORA, MLA_HEADS, MLA_NOPE + MLA_V)[..., MLA_NOPE:]
    wv = jnp.zeros((MLA_HEADS, KV_LORA, MLA_HEADS, MLA_V), F32)
    for h in range(MLA_HEADS):
        wv = wv.at[h, :, h, :].set(w3v[:, h, :])
    wv = wv.reshape(MLA_HEADS * KV_LORA, MLA_HEADS * MLA_V).astype(BF16)
    wom = w_out[:MLA_HEADS * MLA_V].astype(BF16)
    wos = w_out[MLA_HEADS * MLA_V:].astype(BF16)
    row = lambda w: pl.BlockSpec((tm, w), lambda i: (i, 0))
    return pl.pallas_call(
        _post_sample_kernel,
        grid=(n // tm,),
        in_specs=[row(d), row(olat.shape[1]), row(osb.shape[1]), _const_spec(wv.shape), _const_spec(wom.shape),
                  _const_spec(wos.shape)] + _ffn_specs(d, wg.shape[1]),
        out_specs=row(d),
        out_shape=jax.ShapeDtypeStruct((n, d), F32),
        compiler_params=_cparams(("parallel",)),
        name="post_sample",
    )(x2, olat, osb, wv, wom, wos, g_post.reshape(1, d), g_fpre.reshape(1, d), g_fpost.reshape(1, d), wg, wu, wd)


HALO = 16


def _pool_prompt_kernel(x_ref, xh_ref, gm_ref, wp_ref, ps_ref, gp_ref, gf_ref, gq_ref, wg_ref, wu_ref, wd_ref,
                        y_ref, hist_ref, ext_s, *, tm, tiles_per_seq):
    i = pl.program_id(0)
    t_in_seq = i % tiles_per_seq
    x = x_ref[...]
    gm = gm_ref[...]
    hp = _rms(x, gm)
    halo = _rms(xh_ref[...], gm)
    halo = jnp.where(t_in_seq == 0, 0.0, halo)
    ext_s[0:HALO, :] = halo
    ext_s[HALO:HALO + tm, :] = hp
    hist_ref[...] = hp[tm - HALO:, :]
    grp = x.shape[1] // len(POOL_WINDOWS)
    pos = t_in_seq * tm + lax.broadcasted_iota(jnp.int32, (tm, 1), 0)
    outs = []
    for g, w in enumerate(POOL_WINDOWS):
        cs = slice(g * grp, (g + 1) * grp)
        cur = ext_s[HALO:HALO + tm, cs]
        tot = cur
        for k in range(1, w):
            tot = tot + ext_s[HALO - k:HALO - k + tm, cs]
        count = jnp.minimum(pos + 1, w).astype(F32)
        dlt = (tot / count - cur).astype(BF16)
        outs.append(_dot(dlt, wp_ref[g]))
    mix = jnp.concatenate(outs, axis=-1) * ps_ref[...]
    y_ref[...] = _ffn_tail(x, mix, gp_ref[...], gf_ref[...], gq_ref[...], wg_ref, wu_ref, wd_ref)


def _pool_prompt(x2, seq_len, g_mix, w_pool, pool_scale, g_post, g_fpre, g_fpost, wg, wu, wd, tm):
    n, d = x2.shape
    tiles_per_seq = seq_len // tm
    nseq = n // seq_len
    hb = tm // HALO
    row = pl.BlockSpec((tm, d), lambda i: (i, 0))
    halo = pl.BlockSpec((HALO, d), lambda i: (jnp.maximum(i * hb - 1, 0), 0))
    hist = pl.BlockSpec((None, HALO, d), lambda i: (i // tiles_per_seq, 0, 0))
    return pl.pallas_call(
        functools.partial(_pool_prompt_kernel, tm=tm, tiles_per_seq=tiles_per_seq),
        grid=(n // tm,),
        in_specs=[row, halo, _const_spec((1, d)), _const_spec(w_pool.shape), _const_spec((1, d))]
        + _ffn_specs(d, wg.shape[1]),
        out_specs=[row, hist],
        out_shape=[jax.ShapeDtypeStruct((n, d), F32), jax.ShapeDtypeStruct((nseq, HALO, d), F32)],
        scratch_shapes=[pltpu.VMEM((HALO + tm, d), F32)],
        compiler_params=_cparams(("arbitrary",)),
        name="pool_prompt",
    )(x2, x2, g_mix.reshape(1, d), w_pool.astype(BF16), pool_scale.reshape(1, d),
      g_post.reshape(1, d), g_fpre.reshape(1, d), g_fpost.reshape(1, d), wg, wu, wd)


def _pool_sample_kernel(x_ref, st_ref, gm_ref, wp_ref, ps_ref, gp_ref, gf_ref, gq_ref, wg_ref, wu_ref, wd_ref,
                        y_ref, hs_ref, *, t_new):
    nb, d = x_ref.shape[1], x_ref.shape[2]
    gm = gm_ref[...]
    x = x_ref[...]
    hs = _rms(x, gm)
    hs_ref[...] = hs
    ext = [st_ref[k] for k in range(POOL_HIST)] + [hs[t] for t in range(t_new)]
    grp = d // len(POOL_WINDOWS)
    outs = []
    for g, w in enumerate(POOL_WINDOWS):
        cs = slice(g * grp, (g + 1) * grp)
        dl = []
        for t in range(t_new):
            cur = ext[POOL_HIST + t][:, cs]
            tot = cur
            for k in range(1, w):
                tot = tot + ext[POOL_HIST + t - k][:, cs]
            dl.append(tot / float(w) - cur)
        dlt = jnp.concatenate(dl, axis=0).astype(BF16)
        outs.append(_dot(dlt, wp_ref[g]))
    mix = jnp.concatenate(outs, axis=-1) * ps_ref[...]
    y = _ffn_tail(x.reshape(t_new * nb, d), mix, gp_ref[...], gf_ref[...], gq_ref[...], wg_ref, wu_ref, wd_ref)
    y_ref[...] = y.reshape(t_new, nb, d)


def _pool_sample(x_t, st_t, g_mix, w_pool, pool_scale, g_post, g_fpre, g_fpost, wg, wu, wd, nb):
    t_new, b, d = x_t.shape
    slab = lambda r: pl.BlockSpec((r, nb, d), lambda i: (0, i, 0))
    return pl.pallas_call(
        functools.partial(_pool_sample_kernel, t_new=t_new),
        grid=(b // nb,),
        in_specs=[slab(t_new), slab(POOL_HIST), _const_spec((1, d)), _const_spec(w_pool.shape),
                  _const_spec((1, d))] + _ffn_specs(d, wg.shape[1]),
        out_specs=[slab(t_new), slab(t_new)],
        out_shape=[jax.ShapeDtypeStruct((t_new, b, d), F32), jax.ShapeDtypeStruct((t_new, b, d), F32)],
        compiler_params=_cparams(("parallel",)),
        name="pool_sample",
    )(x_t, st_t, g_mix.reshape(1, d), w_pool.astype(BF16), pool_scale.reshape(1, d),
      g_post.reshape(1, d), g_fpre.reshape(1, d), g_fpost.reshape(1, d), wg, wu, wd)


def kernel(x_prompt, x_sample, cache_ckv, cache_kpe, cache_sb_k, cache_sb_v, state_pool, page_table,
           ln_mix_pre, ln_mix_post, ln_ffn_pre, ln_ffn_post, w_in, q_norm, w_uq, kv_norm, w_ukv, w_out,
           w_pool, pool_scale, w_gate, w_up, w_down):
    bsz, seq, d = x_prompt.shape
    nb, t_new, _ = x_sample.shape
    n_pages = page_table.shape[1]
    past = n_pages * PAGE_SIZE
    n_p, n_s = bsz * seq, nb * t_new
    tm_p = min(512, seq)
    tm_s = min(512, n_s)
    tq = min(256, seq // 2)

    wg = w_gate.astype(BF16)
    wu = w_up.astype(BF16)
    wd = w_down.astype(BF16)
    xp = x_prompt.reshape(n_p, d)
    xs = x_sample.reshape(n_s, d)
    pos_p = jnp.tile(jnp.arange(seq, dtype=F32), bsz)
    pos_s = jnp.tile(past + jnp.arange(t_new, dtype=F32), nb)

    lw = (ln_mix_pre[0], w_in[0], q_norm[0], w_uq[0], kv_norm[0], w_ukv[0])
    qm, qs, km, ks, vv, ckv_p, kpe_p, skp_p, svp_p = _proj_prompt(xp, pos_p, *lw, tm=min(256, seq))
    sh = (bsz, seq, MLA_HEADS * LANES)
    o_p = _attn_prompt(qm.reshape(sh), qs.reshape(sh), km.reshape(sh), ks.reshape(sh), vv.reshape(sh), tq, hg=2)
    tail0 = (ln_mix_post[0], ln_ffn_pre[0], ln_ffn_post[0], wg[0], wu[0], wd[0])
    yp = _post_prompt(xp, o_p.reshape(n_p, -1), w_out[0], *tail0, tm=tm_p)

    qf, ql, sq, ckv_s, kpe_s, sk_s, sv_s = _proj_sample(xs, pos_s, *lw, tm=min(256, n_s))
    kpe_s32 = kpe_s[:, MLA_NOPE:MLA_NOPE + MLA_ROPE]
    n_phys = cache_ckv.shape[1]
    nl = cache_sb_k.shape[0]
    kpe_t = jnp.swapaxes(cache_kpe, 2, 3)
    sbk_t = jnp.transpose(cache_sb_k, (0, 1, 3, 4, 2)).reshape(nl, n_phys, SB_WIDTH, PAGE_SIZE)
    sbv_t = jnp.transpose(cache_sb_v, (0, 1, 3, 4, 2)).reshape(nl, n_phys, SB_WIDTH, PAGE_SIZE)
    olat, osb = _attn_sample(page_table, qf, ql, sq, ckv_s, kpe_s32, sk_s, sv_s,
                             cache_ckv, kpe_t, sbk_t, sbv_t, 0, t_new)
    ys = _post_sample(xs, olat, osb, w_ukv[0], w_out[0], *tail0, tm=tm_s)

    tail1 = (ln_mix_post[1], ln_ffn_pre[1], ln_ffn_post[1], wg[1], wu[1], wd[1])
    yp, hist_p = _pool_prompt(yp, seq, ln_mix_pre[1], w_pool[0], pool_scale[0], *tail1, tm=tm_p)
    ys_t = jnp.swapaxes(ys.reshape(nb, t_new, d), 0, 1)
    st_t = jnp.swapaxes(state_pool[0], 0, 1)
    ys_t, hs_t = _pool_sample(ys_t, st_t, ln_mix_pre[1], w_pool[0], pool_scale[0], *tail1, nb=min(64, nb))
    ys = jnp.swapaxes(ys_t, 0, 1)
    hs = jnp.swapaxes(hs_t, 0, 1)

    def unframe(x, n_rows, off):
        return x.reshape(n_rows, SB_HEADS, LANES)[:, :, off:off + SB_HEAD_DIM]

    new_pool_s = jnp.concatenate([state_pool[0], hs], axis=1)[:, t_new:]
    return (yp.reshape(bsz, seq, d), ys,
            ckv_p.reshape(1, bsz, seq, KV_LORA),
            kpe_p[:, MLA_NOPE:MLA_NOPE + MLA_ROPE].reshape(1, bsz, seq, MLA_ROPE),
            unframe(skp_p, n_p, 0).reshape(1, bsz, seq, SB_HEADS, SB_HEAD_DIM),
            unframe(svp_p, n_p, 0).reshape(1, bsz, seq, SB_HEADS, SB_HEAD_DIM),
            hist_p[:, HALO - POOL_HIST:][None],
            ckv_s.reshape(1, nb, t_new, KV_LORA),
            kpe_s32.reshape(1, nb, t_new, MLA_ROPE),
            sk_s.reshape(1, nb, t_new, SB_HEADS, SB_HEAD_DIM),
            sv_s.reshape(1, nb, t_new, SB_HEADS, SB_HEAD_DIM),
            new_pool_s[None])
```

```python
import functools
import math

import jax
import jax.numpy as jnp
from jax import lax
from jax.experimental import pallas as pl
from jax.experimental.pallas import tpu as pltpu

F32 = jnp.float32
BF16 = jnp.bfloat16

MLA_HEADS = 8
MLA_NOPE = 64
MLA_ROPE = 32
MLA_V = 64
Q_LORA = 384
KV_LORA = 256
SB_HEADS = 8
SB_HEAD_DIM = 64
SB_WIDTH = SB_HEADS * SB_HEAD_DIM
PAGE_SIZE = 128
ROPE_THETA = 10000.0
POOL_WINDOWS = (2, 4, 8, 16)
POOL_HIST = 15
RMS_EPS = 1e-6

LOG2E = math.log2(math.e)
MLA_QSCALE = (MLA_NOPE + MLA_ROPE) ** -0.5 * LOG2E
SB_QSCALE = SB_HEAD_DIM ** -0.5 * LOG2E
SB_DONE = 150.0
MLA_CHUNK_PAGES = 32

LANES = 128
HALF = LANES // 2
VMEM_LIMIT = 56 * 1024 * 1024


def _cparams(sem):
    return pltpu.CompilerParams(dimension_semantics=sem, vmem_limit_bytes=VMEM_LIMIT)


def _const_spec(shape):
    nd = len(shape)
    return pl.BlockSpec(shape, lambda *_: (0,) * nd, pipeline_mode=pl.Buffered(1))


def _rms(x, g):
    ms = jnp.mean(x * x, axis=-1, keepdims=True)
    return x * lax.rsqrt(ms + RMS_EPS) * g


def _dot(a, b):
    return jnp.dot(a, b, preferred_element_type=F32)


def _dot_nt(a, b):
    return lax.dot_general(a, b, (((1,), (1,)), ((), ())), preferred_element_type=F32)


def _head_lane_mask(shape):
    lane = lax.broadcasted_iota(jnp.int32, shape, len(shape) - 1)
    return (lane % LANES) < HALF


_P_CQ, _P_CKV, _P_KPA, _P_KPB, _P_SQ = 0, 384, 640, 768, 896
_P_PSK, _P_PEND = 896, 1408
_S_SQ, _S_SK, _S_SV, _S_END = 896, 1408, 1920, 2432


def _proj_common(x_ref, g_ref, win_ref, qn_ref, kvn_ref, wa_ref, wb_ref, cos_ref, sin_ref):
    h = _rms(x_ref[...], g_ref[...]).astype(BF16)
    z = _dot(h, win_ref[...])
    cq = _rms(z[:, _P_CQ:_P_CKV], qn_ref[...]).astype(BF16)
    ckv = _rms(z[:, _P_CKV:_P_KPA], kvn_ref[...])
    cos = cos_ref[...]
    sin = sin_ref[...]
    kpe = z[:, _P_KPA:_P_KPB] * cos + z[:, _P_KPB:_P_SQ] * sin
    cos8 = jnp.tile(cos, (1, MLA_HEADS))
    sin8 = jnp.tile(sin, (1, MLA_HEADS))
    q = (_dot(cq, wa_ref[...]) * cos8 + _dot(cq, wb_ref[...]) * sin8) * MLA_QSCALE
    return h, z, ckv, kpe, q


def _proj_prompt_kernel(x_ref, g_ref, win_ref, qn_ref, kvn_ref, wukv_ref, wat_ref, wbt_ref, wsqt_ref, wskvt_ref,
                        wukvt_ref, cos_ref, sin_ref, cost_ref, sint_ref,
                        qmt_ref, qst_ref, km_ref, ks_ref, vvt_ref, ckv_ref, kpe_ref, skt_ref, svt_ref):
    h = _rms(x_ref[...], g_ref[...]).astype(BF16)
    z = _dot(h, win_ref[...])
    cq = _rms(z[:, _P_CQ:_P_CKV], qn_ref[...]).astype(BF16)
    ckv = _rms(z[:, _P_CKV:_P_KPA], kvn_ref[...])
    ckv_b = ckv.astype(BF16)
    kpe = z[:, _P_KPA:_P_KPB] * cos_ref[...] + z[:, _P_KPB:_P_PSK] * sin_ref[...]
    kv = _dot(ckv_b, wukv_ref[...])
    km_ref[...] = jnp.where(_head_lane_mask(kv.shape), kv, jnp.tile(kpe, (1, MLA_HEADS))).astype(BF16)
    ks_ref[...] = z[:, _P_PSK:_P_PEND].astype(BF16)
    ckv_ref[...] = ckv
    kpe_ref[...] = kpe

    cos8 = jnp.tile(cost_ref[...], (MLA_HEADS, 1))
    sin8 = jnp.tile(sint_ref[...], (MLA_HEADS, 1))
    qt = (_dot_nt(wat_ref[...], cq) * cos8 + _dot_nt(wbt_ref[...], cq) * sin8) * MLA_QSCALE
    qmt_ref[...] = qt.astype(BF16)
    sqt = _dot_nt(wsqt_ref[...], h) * SB_QSCALE
    zero = jnp.zeros((SB_HEAD_DIM, sqt.shape[1]), F32)
    parts = []
    for hd in range(SB_HEADS):
        blk = sqt[hd * SB_HEAD_DIM:(hd + 1) * SB_HEAD_DIM]
        parts += [blk, zero] if hd % 2 == 0 else [zero, blk]
    qst_ref[...] = jnp.concatenate(parts, axis=0).astype(BF16)
    skv_t = _dot_nt(wskvt_ref[...], h)
    skt_ref[...] = skv_t[:SB_WIDTH]
    svt_ref[...] = skv_t[SB_WIDTH:]
    kv_t = _dot_nt(wukvt_ref[...], ckv_b)
    parts = []
    for hd in range(MLA_HEADS):
        parts += [skv_t[SB_WIDTH + hd * SB_HEAD_DIM:SB_WIDTH + (hd + 1) * SB_HEAD_DIM],
                  kv_t[hd * LANES + MLA_NOPE:(hd + 1) * LANES]]
    vvt_ref[...] = jnp.concatenate(parts, axis=0).astype(BF16)


def _proj_sample_kernel(x_ref, g_ref, win_ref, qn_ref, kvn_ref, wa_ref, wb_ref, wabs_ref, cos_ref, sin_ref,
                        qf_ref, ql_ref, sq_ref, ckv_ref, kpe_ref, sk_ref, sv_ref):
    h, z, ckv, kpe, q = _proj_common(x_ref, g_ref, win_ref, qn_ref, kvn_ref, wa_ref, wb_ref, cos_ref, sin_ref)
    qf_ref[...] = q
    ql_ref[...] = _dot(q.astype(BF16), wabs_ref[...])
    sq_ref[...] = z[:, _S_SQ:_S_SK] * SB_QSCALE
    ckv_ref[...] = ckv
    kpe_ref[...] = kpe
    sk_ref[...] = z[:, _S_SK:_S_SV]
    sv_ref[...] = z[:, _S_SV:_S_END]


def _frame(w, d, off):
    k = w.shape[0]
    w3 = w.reshape(k, -1, d)
    out = jnp.zeros((k, w3.shape[1], LANES), w.dtype)
    out = out.at[:, :, off:off + d].set(w3)
    return out.reshape(k, -1)


def _swap_halves(w, d):
    k = w.shape[0]
    w3 = w.reshape(k, -1, d)
    return jnp.concatenate([w3[..., d // 2:], w3[..., :d // 2]], axis=-1).reshape(k, -1)


def _rope_tables(pos):
    half = MLA_ROPE // 2
    inv = ROPE_THETA ** (-jnp.arange(half, dtype=F32) / half)
    ang = pos[:, None] * inv[None, :]
    cos, sin = jnp.cos(ang), jnp.sin(ang)
    n = pos.shape[0]
    one = jnp.ones((n, MLA_NOPE), F32)
    zero_n = jnp.zeros((n, MLA_NOPE), F32)
    pad = jnp.zeros((n, LANES - MLA_NOPE - MLA_ROPE), F32)
    return (jnp.concatenate([one, cos, cos, pad], axis=1),
            jnp.concatenate([zero_n, -sin, sin, pad], axis=1))


def _proj_weights(w_in, w_uq, sample):
    o1 = Q_LORA
    o2 = o1 + KV_LORA
    o3 = o2 + MLA_ROPE
    o4 = o3 + SB_WIDTH
    o5 = o4 + SB_WIDTH
    w_kpe = w_in[:, o2:o3]
    cols = [w_in[:, :o2], _frame(w_kpe, MLA_ROPE, MLA_NOPE), _frame(_swap_halves(w_kpe, MLA_ROPE), MLA_ROPE, MLA_NOPE)]
    cols += [w_in[:, o3:]] if sample else [w_in[:, o4:o5]]
    win = jnp.concatenate(cols, axis=1).astype(BF16)
    dq = MLA_NOPE + MLA_ROPE
    wq3 = w_uq.reshape(Q_LORA, MLA_HEADS, dq)
    wa = _frame(w_uq, dq, 0)
    pe_sw = _swap_halves(wq3[..., MLA_NOPE:].reshape(Q_LORA, -1), MLA_ROPE)
    wb = _frame(pe_sw, MLA_ROPE, MLA_NOPE)
    return win, wa.astype(BF16), wb.astype(BF16)


def _proj_prompt(x2, seq_len, pos, g, w_in, q_norm, w_uq, kv_norm, w_ukv, tm):
    n, d = x2.shape
    win, wa, wb = _proj_weights(w_in, w_uq, sample=False)
    o_sq = Q_LORA + KV_LORA + MLA_ROPE
    wsq_t = w_in[:, o_sq:o_sq + SB_WIDTH].T.astype(BF16)
    wskv_t = w_in[:, o_sq + SB_WIDTH:].T.astype(BF16)
    wukv = w_ukv.astype(BF16)
    cos, sin = _rope_tables(pos)
    hw = MLA_HEADS * LANES
    tps = seq_len // tm
    nseq = n // seq_len
    row = lambda w: pl.BlockSpec((tm, w), lambda i: (i, 0))
    slab = pl.BlockSpec((None, None, hw, tm), lambda i: (i // tps, i % tps, 0, 0))
    col = pl.BlockSpec((None, SB_WIDTH, tm), lambda i: (i // tps, 0, i % tps))
    tcol = pl.BlockSpec((LANES, tm), lambda i: (0, i))
    slab_t = jax.ShapeDtypeStruct((nseq, tps, hw, tm), BF16)
    skv_t = jax.ShapeDtypeStruct((nseq, SB_WIDTH, seq_len), F32)
    outs = [slab_t, slab_t, jax.ShapeDtypeStruct((n, hw), BF16), jax.ShapeDtypeStruct((n, SB_WIDTH), BF16), slab_t,
            jax.ShapeDtypeStruct((n, KV_LORA), F32), jax.ShapeDtypeStruct((n, LANES), F32), skv_t, skv_t]
    return pl.pallas_call(
        _proj_prompt_kernel,
        grid=(n // tm,),
        in_specs=[row(d), _const_spec((1, d)), _const_spec(win.shape), _const_spec((1, Q_LORA)),
                  _const_spec((1, KV_LORA)), _const_spec(wukv.shape), _const_spec((hw, Q_LORA)),
                  _const_spec((hw, Q_LORA)), _const_spec(wsq_t.shape), _const_spec(wskv_t.shape),
                  _const_spec((hw, KV_LORA)), row(LANES), row(LANES), tcol, tcol],
        out_specs=[slab, slab, row(hw), row(SB_WIDTH), slab, row(KV_LORA), row(LANES), col, col],
        out_shape=outs,
        compiler_params=_cparams(("parallel",)),
        name="proj_prompt",
    )(x2, g.reshape(1, d), win, q_norm.reshape(1, -1), kv_norm.reshape(1, -1), wukv, wa.T, wb.T,
      wsq_t, wskv_t, wukv.T, cos, sin, cos.T, sin.T)


def _proj_sample(x2, pos, g, w_in, q_norm, w_uq, kv_norm, w_ukv, tm):
    n, d = x2.shape
    win, wa, wb = _proj_weights(w_in, w_uq, sample=True)
    cos, sin = _rope_tables(pos)
    hw = MLA_HEADS * LANES
    w3k = w_ukv.reshape(KV_LORA, MLA_HEADS, MLA_NOPE + MLA_V)[..., :MLA_NOPE]
    wabs = jnp.zeros((MLA_HEADS, LANES, MLA_HEADS, KV_LORA), F32)
    for h in range(MLA_HEADS):
        wabs = wabs.at[h, :MLA_NOPE, h, :].set(w3k[:, h, :].T)
    wabs = wabs.reshape(hw, MLA_HEADS * KV_LORA).astype(BF16)
    row = lambda w: pl.BlockSpec((tm, w), lambda i: (i, 0))
    outs = [jax.ShapeDtypeStruct((n, hw), F32), jax.ShapeDtypeStruct((n, MLA_HEADS * KV_LORA), F32),
            jax.ShapeDtypeStruct((n, SB_WIDTH), F32), jax.ShapeDtypeStruct((n, KV_LORA), F32),
            jax.ShapeDtypeStruct((n, LANES), F32), jax.ShapeDtypeStruct((n, SB_WIDTH), F32),
            jax.ShapeDtypeStruct((n, SB_WIDTH), F32)]
    return pl.pallas_call(
        _proj_sample_kernel,
        grid=(n // tm,),
        in_specs=[row(d), _const_spec((1, d)), _const_spec(win.shape), _const_spec((1, Q_LORA)),
                  _const_spec((1, KV_LORA)), _const_spec(wa.shape), _const_spec(wb.shape),
                  _const_spec(wabs.shape), row(LANES), row(LANES)],
        out_specs=[row(hw), row(MLA_HEADS * KV_LORA), row(SB_WIDTH), row(KV_LORA), row(LANES),
                   row(SB_WIDTH), row(SB_WIDTH)],
        out_shape=outs,
        compiler_params=_cparams(("parallel",)),
        name="proj_sample",
    )(x2, g.reshape(1, d), win, q_norm.reshape(1, -1), kv_norm.reshape(1, -1), wa, wb, wabs, cos, sin)


def _softmax_block(s, v, m, l, acc):
    m_new = jnp.maximum(m, jnp.max(s, axis=-1, keepdims=True))
    alpha = jnp.exp2(m - m_new)
    p = jnp.exp2(s - m_new)
    l = alpha * l + jnp.sum(p, axis=-1, keepdims=True)
    acc = alpha * acc + _dot(p.astype(BF16), v)
    return m_new, l, acc


def _stick_block(z, v, tri, c, acc, valid=None, v_transposed=False):
    e = jnp.exp2(-jnp.abs(z))
    lp = jnp.log(1.0 + e) * LOG2E
    mx = jnp.maximum(z, 0.0)
    ls = (z - mx) - lp
    nl = mx + lp
    if valid is not None:
        nl = jnp.where(valid, nl, 0.0)
    after = _dot(nl.astype(BF16), tri)
    a = jnp.exp2(ls - after - c)
    if valid is not None:
        a = jnp.where(valid, a, 0.0)
    a = a.astype(BF16)
    acc = acc + (_dot_nt(a, v) if v_transposed else _dot(a, v))
    c = c + jnp.sum(nl, axis=-1, keepdims=True)
    return c, acc


def _tri(k):
    r = lax.broadcasted_iota(jnp.int32, (k, k), 0)
    s = lax.broadcasted_iota(jnp.int32, (k, k), 1)
    return jnp.where(r > s, 1.0, 0.0).astype(BF16)


def _all_done(cs):
    cmin = cs[0]
    for c in cs[1:]:
        cmin = jnp.minimum(cmin, c)
    return (jnp.min(cmin) > SB_DONE).astype(jnp.int32)


def _softmax_block_t(st, vt, m, l, acc):
    m_new = jnp.maximum(m, jnp.max(st, axis=0, keepdims=True))
    alpha = jnp.exp2(m - m_new)
    p = jnp.exp2(st - m_new)
    l = alpha * l + jnp.sum(p, axis=0, keepdims=True)
    acc = alpha * acc + _dot(vt, p.astype(BF16))
    return m_new, l, acc


def _stick_block_t(zt, vt, tri_t, c, acc, valid=None):
    e = jnp.exp2(-jnp.abs(zt))
    lp = jnp.log(1.0 + e) * LOG2E
    mx = jnp.maximum(zt, 0.0)
    ls = (zt - mx) - lp
    nl = mx + lp
    if valid is not None:
        nl = jnp.where(valid, nl, 0.0)
    after = _dot(tri_t, nl.astype(BF16))
    a = jnp.exp2(ls - after - c)
    if valid is not None:
        a = jnp.where(valid, a, 0.0)
    acc = acc + _dot(vt, a.astype(BF16))
    c = c + jnp.sum(nl, axis=0, keepdims=True)
    return c, acc


def _attn_prompt_kernel(qmt_ref, qst_ref, km_ref, ks_ref, vvt_ref, o_ref, *, tq, kb):
    i = pl.program_id(2)
    heads = range(2)
    hs = [pl.ds(h * LANES, LANES) for h in heads]
    qmt = [qmt_ref[hs[h], :] for h in heads]
    qst = [qst_ref[hs[h], :] for h in heads]
    tk = kb * tq

    def keys(start, size):
        return pl.ds(pl.multiple_of(start, tq), size)

    def values(h, blk0, nblk):
        return jnp.concatenate([vvt_ref[blk0 + j, hs[h], :] for j in range(nblk)], axis=1)

    n_full = i // kb

    def scores(p):
        r = keys(p * tk, tk)
        return tuple(_dot(km_ref[r, hs[h]], qmt[h]) for h in heads)

    def mla_body(p, carry):
        s_cur, state = carry
        s_next = scores(p + 1)
        state = tuple(_softmax_block_t(s_cur[h], values(h, p * kb, kb), *state[h]) for h in heads)
        return s_next, state

    init = tuple((jnp.full((1, tq), -jnp.inf, F32), jnp.zeros((1, tq), F32), jnp.zeros((LANES, tq), F32))
                 for _ in heads)
    s_diag, state = lax.fori_loop(0, n_full, mla_body, (scores(0), init))
    kpos = n_full * tk + lax.broadcasted_iota(jnp.int32, (tk, tq), 0)
    qpos = i * tq + lax.broadcasted_iota(jnp.int32, (tk, tq), 1)
    causal = kpos <= qpos
    state = tuple(_softmax_block_t(jnp.where(causal, s_diag[h], -jnp.inf), values(h, n_full * kb, kb), *state[h])
                  for h in heads)

    j0 = jnp.maximum(i - 1, 0)
    krow2 = lax.broadcasted_iota(jnp.int32, (2 * tq, 2 * tq), 0)
    kcol2 = lax.broadcasted_iota(jnp.int32, (2 * tq, 2 * tq), 1)
    tri2_t = jnp.where(kcol2 > krow2, 1.0, 0.0).astype(BF16)
    kpos = j0 * tq + lax.broadcasted_iota(jnp.int32, (2 * tq, tq), 0)
    qpos = i * tq + lax.broadcasted_iota(jnp.int32, (2 * tq, tq), 1)
    ks_first = ks_ref[keys(j0 * tq, 2 * tq), :]
    sb = tuple(_stick_block_t(_dot(ks_first, qst[h]), values(h, j0, 2), tri2_t,
                              jnp.zeros((1, tq), F32), jnp.zeros((LANES, tq), F32), valid=kpos < qpos)
               for h in heads)

    def sb_cond(carry):
        j, done, _ = carry
        return jnp.logical_and(j >= 0, done == 0)

    def sb_body(carry):
        j, _, sb = carry
        krow = lax.broadcasted_iota(jnp.int32, (tq, tq), 0)
        kcol = lax.broadcasted_iota(jnp.int32, (tq, tq), 1)
        tri_t = jnp.where(kcol > krow, 1.0, 0.0).astype(BF16)
        ks_blk = ks_ref[keys(j * tq, tq), :]
        sb = tuple(_stick_block_t(_dot(ks_blk, qst[h]), values(h, j, 1), tri_t, *sb[h]) for h in heads)
        return j - 1, _all_done([st[0] for st in sb]), sb

    _, _, sb = lax.while_loop(sb_cond, sb_body, (j0 - 1, _all_done([st[0] for st in sb]), sb))

    top = lax.broadcasted_iota(jnp.int32, (LANES, tq), 0) < HALF
    for h in heads:
        _, l, acc_mla = state[h]
        o_t = jnp.where(top, sb[h][1], acc_mla / l)
        o_ref[:, hs[h]] = o_t.T.astype(o_ref.dtype)


def _attn_prompt(qmt, qst, km, ks, vvt, kb):
    b, nblk, hw, tq = qmt.shape
    s = nblk * tq
    pair = 2 * LANES
    assert hw % pair == 0 and nblk % kb == 0 and 2 * SB_HEAD_DIM == LANES
    qspec = pl.BlockSpec((None, None, pair, tq), lambda bi, g, i: (bi, i, g, 0))
    return pl.pallas_call(
        functools.partial(_attn_prompt_kernel, tq=tq, kb=kb),
        grid=(b, hw // pair, nblk),
        in_specs=[qspec, qspec,
                  pl.BlockSpec((None, s, pair), lambda bi, g, i: (bi, 0, g)),
                  pl.BlockSpec((None, s, LANES), lambda bi, g, i: (bi, 0, g)),
                  pl.BlockSpec((None, nblk, pair, tq), lambda bi, g, i: (bi, 0, g, 0))],
        out_specs=pl.BlockSpec((None, tq, pair), lambda bi, g, i: (bi, i, g)),
        out_shape=jax.ShapeDtypeStruct((b, s, hw), BF16),
        compiler_params=_cparams(("parallel", "parallel", "arbitrary")),
        name="attn_prompt",
    )(qmt, qst, km, ks, vvt)


def _attn_sample_kernel(pt_ref, qf_ref, ql_ref, sq_ref, ckvn_ref, kpen_ref, skn_ref, svn_ref,
                        ckv_hbm, kpe_hbm, sbk_hbm, sbv_hbm, olat_ref, osb_ref,
                        ckv_buf, kpe_buf, sbk_buf, sbv_buf, ckv_sem, kpe_sem, sbk_sem, sbv_sem,
                        *, t_new, layer, n_pages, chunk_pages):
    b = pl.program_id(0)
    nb = pl.num_programs(0)
    nh = MLA_HEADS
    rows = nh * t_new
    page = PAGE_SIZE
    n_chunks = n_pages // chunk_pages
    n_pairs = n_chunks // 2

    def ckv_copy(pg, slot, p):
        return pltpu.make_async_copy(ckv_hbm.at[layer, pg], ckv_buf.at[slot, pl.ds(p * page, page), :],
                                     ckv_sem.at[slot])

    def kpe_copy(pg, slot, p):
        return pltpu.make_async_copy(kpe_hbm.at[layer, pg], kpe_buf.at[slot, :, pl.ds(p * page, page)],
                                     kpe_sem.at[slot])

    def sbk_copy(pg, slot):
        return pltpu.make_async_copy(sbk_hbm.at[layer, pg], sbk_buf.at[slot], sbk_sem.at[slot])

    def sbv_copy(pg, slot):
        return pltpu.make_async_copy(sbv_hbm.at[layer, pg], sbv_buf.at[slot], sbv_sem.at[slot])

    def start_chunk(seq, c, slot):
        for p in range(chunk_pages):
            pg = pt_ref[seq, c * chunk_pages + p]
            ckv_copy(pg, slot, p).start()
            kpe_copy(pg, slot, p).start()

    def wait_chunk(slot):
        for p in range(chunk_pages):
            ckv_copy(0, slot, p).wait()
            kpe_copy(0, slot, p).wait()

    def start_sb(pg, slot):
        sbk_copy(pg, slot).start()
        sbv_copy(pg, slot).start()

    def wait_sb(slot):
        sbk_copy(0, slot).wait()
        sbv_copy(0, slot).wait()

    @pl.when(b == 0)
    def _():
        start_chunk(0, 0, 0)

    start_sb(pt_ref[b, n_pages - 1], 0)
    start_sb(pt_ref[b, n_pages - 2], 1)

    qf = qf_ref[...]
    qlr = ql_ref[...]
    sq = sq_ref[...]
    ql = jnp.concatenate([qlr[:, h * KV_LORA:(h + 1) * KV_LORA] for h in range(nh)], axis=0).astype(BF16)
    qp = jnp.concatenate(
        [qf[:, h * LANES + MLA_NOPE:h * LANES + MLA_NOPE + MLA_ROPE] for h in range(nh)], axis=0).astype(BF16)
    lane_head = lax.broadcasted_iota(jnp.int32, (rows, SB_WIDTH), 1) // SB_HEAD_DIM
    row_head = lax.broadcasted_iota(jnp.int32, (rows, SB_WIDTH), 0) // t_new
    qbd = jnp.where(lane_head == row_head, jnp.tile(sq, (nh, 1)), 0.0).astype(BF16)

    def pad(x):
        return jnp.concatenate([x, jnp.zeros((page - t_new, x.shape[1]), x.dtype)], axis=0).astype(BF16)

    qidx = lax.broadcasted_iota(jnp.int32, (rows, page), 0) % t_new
    kidx = lax.broadcasted_iota(jnp.int32, (rows, page), 1)
    ckv_n = pad(ckvn_ref[...])
    s = _dot_nt(ql, ckv_n) + _dot_nt(qp, pad(kpen_ref[...]))
    s = jnp.where(kidx <= qidx, s, -jnp.inf)
    mla = _softmax_block(s, ckv_n, jnp.full((rows, 1), -jnp.inf, F32), jnp.zeros((rows, 1), F32),
                         jnp.zeros((rows, KV_LORA), F32))

    def mla_chunk(slot, mla):
        ckv = ckv_buf[slot].astype(BF16)
        s = _dot_nt(ql, ckv) + _dot(qp, kpe_buf[slot].astype(BF16))
        return _softmax_block(s, ckv, *mla)

    def pair_body(t, mla):
        c0 = 2 * t
        start_chunk(b, c0 + 1, 1)
        wait_chunk(0)
        mla = mla_chunk(0, mla)

        @pl.when(c0 + 2 < n_chunks)
        def _():
            start_chunk(b, c0 + 2, 0)

        @pl.when(jnp.logical_and(c0 + 2 >= n_chunks, b + 1 < nb))
        def _():
            start_chunk(b + 1, 0, 0)

        wait_chunk(1)
        return mla_chunk(1, mla)

    _, l, acc = lax.fori_loop(0, n_pairs, pair_body, mla)
    olat = acc / l
    for h in range(nh):
        olat_ref[:, h * KV_LORA:(h + 1) * KV_LORA] = olat[h * t_new:(h + 1) * t_new, :]

    tri = _tri(page)
    sb = _stick_block(_dot_nt(qbd, pad(skn_ref[...])), pad(svn_ref[...]), tri, jnp.zeros((rows, 1), F32),
                      jnp.zeros((rows, SB_WIDTH), F32), valid=kidx < qidx)

    def sb_page(slot, sb):
        return _stick_block(_dot(qbd, sbk_buf[slot].astype(BF16)), sbv_buf[slot].astype(BF16), tri, *sb,
                            v_transposed=True)

    wait_sb(0)
    sb = sb_page(0, sb)
    wait_sb(1)
    sb = sb_page(1, sb)

    def sb_cond(carry):
        j, done, _ = carry
        return jnp.logical_and(j >= 0, done == 0)

    def sb_body(carry):
        j, _, sb = carry
        start_sb(pt_ref[b, j], 0)
        wait_sb(0)
        sb = sb_page(0, sb)
        return j - 1, _all_done([sb[0]]), sb

    _, _, sb = lax.while_loop(sb_cond, sb_body, (jnp.int32(n_pages - 3), _all_done([sb[0]]), sb))
    a_sb = sb[1]
    head_of_lane = lax.broadcasted_iota(jnp.int32, (t_new, SB_WIDTH), 1) // SB_HEAD_DIM
    o = jnp.zeros((t_new, SB_WIDTH), F32)
    for h in range(nh):
        o = o + jnp.where(head_of_lane == h, a_sb[h * t_new:(h + 1) * t_new, :], 0.0)
    osb_ref[...] = o


def _attn_sample(page_table, qf, ql, sq, ckv_new, kpe_new, sk_new, sv_new,
                 cache_ckv, cache_kpe_t, cache_sbk_t, cache_sbv_t, layer, t_new):
    nb, n_pages = page_table.shape
    chunk_pages = min(MLA_CHUNK_PAGES, n_pages // 2)
    assert n_pages >= 2 and n_pages % (2 * chunk_pages) == 0
    seq = lambda w: pl.BlockSpec((t_new, w), lambda b, pt: (b, 0))
    hbm = pl.BlockSpec(memory_space=pl.ANY)
    ck = chunk_pages * PAGE_SIZE
    grid_spec = pltpu.PrefetchScalarGridSpec(
        num_scalar_prefetch=1,
        grid=(nb,),
        in_specs=[seq(MLA_HEADS * LANES), seq(MLA_HEADS * KV_LORA), seq(SB_WIDTH), seq(KV_LORA), seq(MLA_ROPE),
                  seq(SB_WIDTH), seq(SB_WIDTH), hbm, hbm, hbm, hbm],
        out_specs=[seq(MLA_HEADS * KV_LORA), seq(SB_WIDTH)],
        scratch_shapes=[pltpu.VMEM((2, ck, KV_LORA), F32), pltpu.VMEM((2, MLA_ROPE, ck), F32),
                        pltpu.VMEM((2, SB_WIDTH, PAGE_SIZE), F32), pltpu.VMEM((2, SB_WIDTH, PAGE_SIZE), F32),
                        pltpu.SemaphoreType.DMA((2,)), pltpu.SemaphoreType.DMA((2,)),
                        pltpu.SemaphoreType.DMA((2,)), pltpu.SemaphoreType.DMA((2,))],
    )
    n = nb * t_new
    return pl.pallas_call(
        functools.partial(_attn_sample_kernel, t_new=t_new, layer=layer, n_pages=n_pages,
                          chunk_pages=chunk_pages),
        grid_spec=grid_spec,
        out_shape=[jax.ShapeDtypeStruct((n, MLA_HEADS * KV_LORA), F32), jax.ShapeDtypeStruct((n, SB_WIDTH), F32)],
        compiler_params=_cparams(("arbitrary",)),
        name="attn_sample",
    )(page_table, qf, ql, sq, ckv_new, kpe_new, sk_new, sv_new, cache_ckv, cache_kpe_t, cache_sbk_t, cache_sbv_t)


def _ffn_tail(x, mix, g_post, g_fpre, g_fpost, wg_ref, wu_ref, wd_ref):
    y = x + _rms(mix, g_post)
    h = _rms(y, g_fpre).astype(BF16)
    gate = _dot(h, wg_ref[...])
    up = _dot(h, wu_ref[...])
    act = (gate * jax.nn.sigmoid(gate) * up).astype(BF16)
    return y + _rms(_dot(act, wd_ref[...]), g_fpost)


def _post_prompt_kernel(x_ref, a_ref, wo_ref, gp_ref, gf_ref, gq_ref, wg_ref, wu_ref, wd_ref, y_ref):
    mix = _dot(a_ref[...], wo_ref[...])
    y_ref[...] = _ffn_tail(x_ref[...], mix, gp_ref[...], gf_ref[...], gq_ref[...], wg_ref, wu_ref, wd_ref)


def _post_sample_kernel(x_ref, olat_ref, osb_ref, wv_ref, wom_ref, wos_ref, gp_ref, gf_ref, gq_ref,
                        wg_ref, wu_ref, wd_ref, y_ref):
    o_mla = _dot(olat_ref[...].astype(BF16), wv_ref[...])
    mix = _dot(o_mla.astype(BF16), wom_ref[...]) + _dot(osb_ref[...].astype(BF16), wos_ref[...])
    y_ref[...] = _ffn_tail(x_ref[...], mix, gp_ref[...], gf_ref[...], gq_ref[...], wg_ref, wu_ref, wd_ref)


def _ffn_specs(d, dff):
    return [_const_spec((1, d)), _const_spec((1, d)), _const_spec((1, d)),
            _const_spec((d, dff)), _const_spec((d, dff)), _const_spec((dff, d))]


def _post_prompt(x2, a2, w_out, g_post, g_fpre, g_fpost, wg, wu, wd, tm):
    n, d = x2.shape
    wm = w_out[:MLA_HEADS * MLA_V].reshape(MLA_HEADS, MLA_V, d)
    ws = w_out[MLA_HEADS * MLA_V:].reshape(SB_HEADS, SB_HEAD_DIM, d)
    wo = jnp.concatenate([ws, wm], axis=1).reshape(-1, d).astype(BF16)
    row = lambda w: pl.BlockSpec((tm, w), lambda i: (i, 0))
    return pl.pallas_call(
        _post_prompt_kernel,
        grid=(n // tm,),
        in_specs=[row(d), row(a2.shape[1]), _const_spec(wo.shape)] + _ffn_specs(d, wg.shape[1]),
        out_specs=row(d),
        out_shape=jax.ShapeDtypeStruct((n, d), F32),
        compiler_params=_cparams(("parallel",)),
        name="post_prompt",
    )(x2, a2, wo, g_post.reshape(1, d), g_fpre.reshape(1, d), g_fpost.reshape(1, d), wg, wu, wd)


def _post_sample(x2, olat, osb, w_ukv, w_out, g_post, g_fpre, g_fpost, wg, wu, wd, tm):
    n, d = x2.shape
    w3v = w_ukv.reshape(KV_LORA, MLA_HEADS, MLA_NOPE + MLA_V)[..., MLA_NOPE:]
    wv = jnp.zeros((MLA_HEADS, KV_LORA, MLA_HEADS, MLA_V), F32)
    for h in range(MLA_HEADS):
        wv = wv.at[h, :, h, :].set(w3v[:, h, :])
    wv = wv.reshape(MLA_HEADS * KV_LORA, MLA_HEADS * MLA_V).astype(BF16)
    wom = w_out[:MLA_HEADS * MLA_V].astype(BF16)
    wos = w_out[MLA_HEADS * MLA_V:].astype(BF16)
    row = lambda w: pl.BlockSpec((tm, w), lambda i: (i, 0))
    return pl.pallas_call(
        _post_sample_kernel,
        grid=(n // tm,),
        in_specs=[row(d), row(olat.shape[1]), row(osb.shape[1]), _const_spec(wv.shape), _const_spec(wom.shape),
                  _const_spec(wos.shape)] + _ffn_specs(d, wg.shape[1]),
        out_specs=row(d),
        out_shape=jax.ShapeDtypeStruct((n, d), F32),
        compiler_params=_cparams(("parallel",)),
        name="post_sample",
    )(x2, olat, osb, wv, wom, wos, g_post.reshape(1, d), g_fpre.reshape(1, d), g_fpost.reshape(1, d), wg, wu, wd)


HALO = 16


def _pool_prompt_kernel(x_ref, xh_ref, gm_ref, wp_ref, ps_ref, gp_ref, gf_ref, gq_ref, wg_ref, wu_ref, wd_ref,
                        y_ref, hist_ref, ext_s, *, tm, tiles_per_seq):
    i = pl.program_id(0)
    t_in_seq = i % tiles_per_seq
    x = x_ref[...]
    gm = gm_ref[...]
    hp = _rms(x, gm)
    halo = _rms(xh_ref[...], gm)
    halo = jnp.where(t_in_seq == 0, 0.0, halo)
    ext_s[0:HALO, :] = halo
    ext_s[HALO:HALO + tm, :] = hp
    hist_ref[...] = hp[tm - HALO:, :]
    grp = x.shape[1] // len(POOL_WINDOWS)
    pos = t_in_seq * tm + lax.broadcasted_iota(jnp.int32, (tm, 1), 0)
    outs = []
    for g, w in enumerate(POOL_WINDOWS):
        cs = slice(g * grp, (g + 1) * grp)
        cur = ext_s[HALO:HALO + tm, cs]
        tot = cur
        for k in range(1, w):
            tot = tot + ext_s[HALO - k:HALO - k + tm, cs]
        count = jnp.minimum(pos + 1, w).astype(F32)
        dlt = (tot / count - cur).astype(BF16)
        outs.append(_dot(dlt, wp_ref[g]))
    mix = jnp.concatenate(outs, axis=-1) * ps_ref[...]
    y_ref[...] = _ffn_tail(x, mix, gp_ref[...], gf_ref[...], gq_ref[...], wg_ref, wu_ref, wd_ref)


def _pool_prompt(x2, seq_len, g_mix, w_pool, pool_scale, g_post, g_fpre, g_fpost, wg, wu, wd, tm):
    n, d = x2.shape
    tiles_per_seq = seq_len // tm
    nseq = n // seq_len
    hb = tm // HALO
    row = pl.BlockSpec((tm, d), lambda i: (i, 0))
    halo = pl.BlockSpec((HALO, d), lambda i: (jnp.maximum(i * hb - 1, 0), 0))
    hist = pl.BlockSpec((None, HALO, d), lambda i: (i // tiles_per_seq, 0, 0))
    return pl.pallas_call(
        functools.partial(_pool_prompt_kernel, tm=tm, tiles_per_seq=tiles_per_seq),
        grid=(n // tm,),
        in_specs=[row, halo, _const_spec((1, d)), _const_spec(w_pool.shape), _const_spec((1, d))]
        + _ffn_specs(d, wg.shape[1]),
        out_specs=[row, hist],
        out_shape=[jax.ShapeDtypeStruct((n, d), F32), jax.ShapeDtypeStruct((nseq, HALO, d), F32)],
        scratch_shapes=[pltpu.VMEM((HALO + tm, d), F32)],
        compiler_params=_cparams(("arbitrary",)),
        name="pool_prompt",
    )(x2, x2, g_mix.reshape(1, d), w_pool.astype(BF16), pool_scale.reshape(1, d),
      g_post.reshape(1, d), g_fpre.reshape(1, d), g_fpost.reshape(1, d), wg, wu, wd)


def _pool_sample_kernel(x_ref, st_ref, gm_ref, wp_ref, ps_ref, gp_ref, gf_ref, gq_ref, wg_ref, wu_ref, wd_ref,
                        y_ref, hs_ref, *, t_new):
    nb, d = x_ref.shape[1], x_ref.shape[2]
    gm = gm_ref[...]
    x = x_ref[...]
    hs = _rms(x, gm)
    hs_ref[...] = hs
    ext = [st_ref[k] for k in range(POOL_HIST)] + [hs[t] for t in range(t_new)]
    grp = d // len(POOL_WINDOWS)
    outs = []
    for g, w in enumerate(POOL_WINDOWS):
        cs = slice(g * grp, (g + 1) * grp)
        dl = []
        for t in range(t_new):
            cur = ext[POOL_HIST + t][:, cs]
            tot = cur
            for k in range(1, w):
                tot = tot + ext[POOL_HIST + t - k][:, cs]
            dl.append(tot / float(w) - cur)
        dlt = jnp.concatenate(dl, axis=0).astype(BF16)
        outs.append(_dot(dlt, wp_ref[g]))
    mix = jnp.concatenate(outs, axis=-1) * ps_ref[...]
    y = _ffn_tail(x.reshape(t_new * nb, d), mix, gp_ref[...], gf_ref[...], gq_ref[...], wg_ref, wu_ref, wd_ref)
    y_ref[...] = y.reshape(t_new, nb, d)


def _pool_sample(x_t, st_t, g_mix, w_pool, pool_scale, g_post, g_fpre, g_fpost, wg, wu, wd, nb):
    t_new, b, d = x_t.shape
    slab = lambda r: pl.BlockSpec((r, nb, d), lambda i: (0, i, 0))
    return pl.pallas_call(
        functools.partial(_pool_sample_kernel, t_new=t_new),
        grid=(b // nb,),
        in_specs=[slab(t_new), slab(POOL_HIST), _const_spec((1, d)), _const_spec(w_pool.shape),
                  _const_spec((1, d))] + _ffn_specs(d, wg.shape[1]),
        out_specs=[slab(t_new), slab(t_new)],
        out_shape=[jax.ShapeDtypeStruct((t_new, b, d), F32), jax.ShapeDtypeStruct((t_new, b, d), F32)],
        compiler_params=_cparams(("parallel",)),
        name="pool_sample",
    )(x_t, st_t, g_mix.reshape(1, d), w_pool.astype(BF16), pool_scale.reshape(1, d),
      g_post.reshape(1, d), g_fpre.reshape(1, d), g_fpost.reshape(1, d), wg, wu, wd)


def kernel(x_prompt, x_sample, cache_ckv, cache_kpe, cache_sb_k, cache_sb_v, state_pool, page_table,
           ln_mix_pre, ln_mix_post, ln_ffn_pre, ln_ffn_post, w_in, q_norm, w_uq, kv_norm, w_ukv, w_out,
           w_pool, pool_scale, w_gate, w_up, w_down):
    bsz, seq, d = x_prompt.shape
    nb, t_new, _ = x_sample.shape
    n_pages = page_table.shape[1]
    past = n_pages * PAGE_SIZE
    n_p, n_s = bsz * seq, nb * t_new
    tm_p = min(512, seq)
    tm_s = min(512, n_s)
    tq = min(256, seq // 2)

    wg = w_gate.astype(BF16)
    wu = w_up.astype(BF16)
    wd = w_down.astype(BF16)
    xp = x_prompt.reshape(n_p, d)
    xs = x_sample.reshape(n_s, d)
    pos_p = jnp.tile(jnp.arange(seq, dtype=F32), bsz)
    pos_s = jnp.tile(past + jnp.arange(t_new, dtype=F32), nb)

    lw = (ln_mix_pre[0], w_in[0], q_norm[0], w_uq[0], kv_norm[0], w_ukv[0])
    qmt, qst, km, ks, vvt, ckv_p, kpe_p, skt_p, svt_p = _proj_prompt(xp, seq, pos_p, *lw, tm=tq)
    o_p = _attn_prompt(qmt, qst, km.reshape(bsz, seq, -1), ks.reshape(bsz, seq, -1), vvt, kb=2)
    tail0 = (ln_mix_post[0], ln_ffn_pre[0], ln_ffn_post[0], wg[0], wu[0], wd[0])
    yp = _post_prompt(xp, o_p.reshape(n_p, -1), w_out[0], *tail0, tm=tm_p)

    qf, ql, sq, ckv_s, kpe_s, sk_s, sv_s = _proj_sample(xs, pos_s, *lw, tm=min(256, n_s))
    kpe_s32 = kpe_s[:, MLA_NOPE:MLA_NOPE + MLA_ROPE]
    n_phys = cache_ckv.shape[1]
    nl = cache_sb_k.shape[0]
    kpe_t = jnp.swapaxes(cache_kpe, 2, 3)
    sbk_t = jnp.transpose(cache_sb_k, (0, 1, 3, 4, 2)).reshape(nl, n_phys, SB_WIDTH, PAGE_SIZE)
    sbv_t = jnp.transpose(cache_sb_v, (0, 1, 3, 4, 2)).reshape(nl, n_phys, SB_WIDTH, PAGE_SIZE)
    olat, osb = _attn_sample(page_table, qf, ql, sq, ckv_s, kpe_s32, sk_s, sv_s,
                             cache_ckv, kpe_t, sbk_t, sbv_t, 0, t_new)
    ys = _post_sample(xs, olat, osb, w_ukv[0], w_out[0], *tail0, tm=tm_s)

    tail1 = (ln_mix_post[1], ln_ffn_pre[1], ln_ffn_post[1], wg[1], wu[1], wd[1])
    yp, hist_p = _pool_prompt(yp, seq, ln_mix_pre[1], w_pool[0], pool_scale[0], *tail1, tm=tm_p)
    ys_t = jnp.swapaxes(ys.reshape(nb, t_new, d), 0, 1)
    st_t = jnp.swapaxes(state_pool[0], 0, 1)
    ys_t, hs_t = _pool_sample(ys_t, st_t, ln_mix_pre[1], w_pool[0], pool_scale[0], *tail1, nb=min(64, nb))
    ys = jnp.swapaxes(ys_t, 0, 1)
    hs = jnp.swapaxes(hs_t, 0, 1)

    new_pool_s = jnp.concatenate([state_pool[0], hs], axis=1)[:, t_new:]
    return (yp.reshape(bsz, seq, d), ys,
            ckv_p.reshape(1, bsz, seq, KV_LORA),
            kpe_p[:, MLA_NOPE:MLA_NOPE + MLA_ROPE].reshape(1, bsz, seq, MLA_ROPE),
            jnp.transpose(skt_p.reshape(1, bsz, SB_HEADS, SB_HEAD_DIM, seq), (0, 1, 4, 2, 3)),
            jnp.transpose(svt_p.reshape(1, bsz, SB_HEADS, SB_HEAD_DIM, seq), (0, 1, 4, 2, 3)),
            hist_p[:, HALO - POOL_HIST:][None],
            ckv_s.reshape(1, nb, t_new, KV_LORA),
            kpe_s32.reshape(1, nb, t_new, MLA_ROPE),
            sk_s.reshape(1, nb, t_new, SB_HEADS, SB_HEAD_DIM),
            sv_s.reshape(1, nb, t_new, SB_HEADS, SB_HEAD_DIM),
            new_pool_s[None])
```

```python
import functools
import math

import jax
import jax.numpy as jnp
from jax import lax
from jax.experimental import pallas as pl
from jax.experimental.pallas import tpu as pltpu

F32 = jnp.float32
BF16 = jnp.bfloat16

MLA_HEADS = 8
MLA_NOPE = 64
MLA_ROPE = 32
MLA_V = 64
Q_LORA = 384
KV_LORA = 256
SB_HEADS = 8
SB_HEAD_DIM = 64
SB_WIDTH = SB_HEADS * SB_HEAD_DIM
PAGE_SIZE = 128
ROPE_THETA = 10000.0
POOL_WINDOWS = (2, 4, 8, 16)
POOL_HIST = 15
RMS_EPS = 1e-6

LOG2E = math.log2(math.e)
MLA_QSCALE = (MLA_NOPE + MLA_ROPE) ** -0.5 * LOG2E
SB_QSCALE = SB_HEAD_DIM ** -0.5 * LOG2E
SB_DONE = 150.0
MLA_CHUNK_PAGES = 64

LANES = 128
HALF = LANES // 2
VMEM_LIMIT = 56 * 1024 * 1024


def _cparams(sem):
    return pltpu.CompilerParams(dimension_semantics=sem, vmem_limit_bytes=VMEM_LIMIT)


def _const_spec(shape):
    nd = len(shape)
    return pl.BlockSpec(shape, lambda *_: (0,) * nd, pipeline_mode=pl.Buffered(1))


def _rms(x, g):
    ms = jnp.mean(x * x, axis=-1, keepdims=True)
    return x * lax.rsqrt(ms + RMS_EPS) * g


def _dot(a, b):
    return jnp.dot(a, b, preferred_element_type=F32)


def _dot_nt(a, b):
    return lax.dot_general(a, b, (((1,), (1,)), ((), ())), preferred_element_type=F32)


def _head_lane_mask(shape):
    lane = lax.broadcasted_iota(jnp.int32, shape, len(shape) - 1)
    return (lane % LANES) < HALF


_P_CQ, _P_CKV, _P_KPA, _P_KPB, _P_SQ = 0, 384, 640, 768, 896
_P_PSK, _P_PEND = 896, 1408
_S_SQ, _S_SK, _S_SV, _S_END = 896, 1408, 1920, 2432


def _proj_common(x_ref, g_ref, win_ref, qn_ref, kvn_ref, wa_ref, wb_ref, cos_ref, sin_ref):
    h = _rms(x_ref[...], g_ref[...]).astype(BF16)
    z = _dot(h, win_ref[...])
    cq = _rms(z[:, _P_CQ:_P_CKV], qn_ref[...]).astype(BF16)
    ckv = _rms(z[:, _P_CKV:_P_KPA], kvn_ref[...])
    cos = cos_ref[...]
    sin = sin_ref[...]
    kpe = z[:, _P_KPA:_P_KPB] * cos + z[:, _P_KPB:_P_SQ] * sin
    cos8 = jnp.tile(cos, (1, MLA_HEADS))
    sin8 = jnp.tile(sin, (1, MLA_HEADS))
    q = (_dot(cq, wa_ref[...]) * cos8 + _dot(cq, wb_ref[...]) * sin8) * MLA_QSCALE
    return h, z, ckv, kpe, q


def _proj_prompt_kernel(x_ref, g_ref, win_ref, qn_ref, kvn_ref, wukv_ref, wat_ref, wbt_ref, wsqt_ref, wskvt_ref,
                        wukvt_ref, cos_ref, sin_ref, cost_ref, sint_ref,
                        qmt_ref, qst_ref, km_ref, ks_ref, vvt_ref, ckv_ref, kpe_ref, skt_ref, svt_ref):
    h = _rms(x_ref[...], g_ref[...]).astype(BF16)
    z = _dot(h, win_ref[...])
    cq = _rms(z[:, _P_CQ:_P_CKV], qn_ref[...]).astype(BF16)
    ckv = _rms(z[:, _P_CKV:_P_KPA], kvn_ref[...])
    ckv_b = ckv.astype(BF16)
    kpe = z[:, _P_KPA:_P_KPB] * cos_ref[...] + z[:, _P_KPB:_P_PSK] * sin_ref[...]
    kv = _dot(ckv_b, wukv_ref[...])
    km_ref[...] = jnp.where(_head_lane_mask(kv.shape), kv, jnp.tile(kpe, (1, MLA_HEADS))).astype(BF16)
    ks_ref[...] = z[:, _P_PSK:_P_PEND].astype(BF16)
    ckv_ref[...] = ckv
    kpe_ref[...] = kpe

    cos8 = jnp.tile(cost_ref[...], (MLA_HEADS, 1))
    sin8 = jnp.tile(sint_ref[...], (MLA_HEADS, 1))
    qt = (_dot_nt(wat_ref[...], cq) * cos8 + _dot_nt(wbt_ref[...], cq) * sin8) * MLA_QSCALE
    qmt_ref[...] = qt.astype(BF16)
    sqt = _dot_nt(wsqt_ref[...], h) * SB_QSCALE
    zero = jnp.zeros((SB_HEAD_DIM, sqt.shape[1]), F32)
    parts = []
    for hd in range(SB_HEADS):
        blk = sqt[hd * SB_HEAD_DIM:(hd + 1) * SB_HEAD_DIM]
        parts += [blk, zero] if hd % 2 == 0 else [zero, blk]
    qst_ref[...] = jnp.concatenate(parts, axis=0).astype(BF16)
    skv_t = _dot_nt(wskvt_ref[...], h)
    skt_ref[...] = skv_t[:SB_WIDTH]
    svt_ref[...] = skv_t[SB_WIDTH:]
    kv_t = _dot_nt(wukvt_ref[...], ckv_b)
    parts = []
    for hd in range(MLA_HEADS):
        parts += [skv_t[SB_WIDTH + hd * SB_HEAD_DIM:SB_WIDTH + (hd + 1) * SB_HEAD_DIM],
                  kv_t[hd * LANES + MLA_NOPE:(hd + 1) * LANES]]
    vvt_ref[...] = jnp.concatenate(parts, axis=0).astype(BF16)


def _proj_sample_kernel(x_ref, g_ref, win_ref, qn_ref, kvn_ref, wa_ref, wb_ref, wabs_ref, cos_ref, sin_ref,
                        qf_ref, ql_ref, sq_ref, ckv_ref, kpe_ref, sk_ref, sv_ref):
    h, z, ckv, kpe, q = _proj_common(x_ref, g_ref, win_ref, qn_ref, kvn_ref, wa_ref, wb_ref, cos_ref, sin_ref)
    qf_ref[...] = q
    ql_ref[...] = _dot(q.astype(BF16), wabs_ref[...])
    sq_ref[...] = z[:, _S_SQ:_S_SK] * SB_QSCALE
    ckv_ref[...] = ckv
    kpe_ref[...] = kpe
    sk_ref[...] = z[:, _S_SK:_S_SV]
    sv_ref[...] = z[:, _S_SV:_S_END]


def _frame(w, d, off):
    k = w.shape[0]
    w3 = w.reshape(k, -1, d)
    out = jnp.zeros((k, w3.shape[1], LANES), w.dtype)
    out = out.at[:, :, off:off + d].set(w3)
    return out.reshape(k, -1)


def _swap_halves(w, d):
    k = w.shape[0]
    w3 = w.reshape(k, -1, d)
    return jnp.concatenate([w3[..., d // 2:], w3[..., :d // 2]], axis=-1).reshape(k, -1)


def _rope_tables(pos):
    half = MLA_ROPE // 2
    inv = ROPE_THETA ** (-jnp.arange(half, dtype=F32) / half)
    ang = pos[:, None] * inv[None, :]
    cos, sin = jnp.cos(ang), jnp.sin(ang)
    n = pos.shape[0]
    one = jnp.ones((n, MLA_NOPE), F32)
    zero_n = jnp.zeros((n, MLA_NOPE), F32)
    pad = jnp.zeros((n, LANES - MLA_NOPE - MLA_ROPE), F32)
    return (jnp.concatenate([one, cos, cos, pad], axis=1),
            jnp.concatenate([zero_n, -sin, sin, pad], axis=1))


def _proj_weights(w_in, w_uq, sample):
    o1 = Q_LORA
    o2 = o1 + KV_LORA
    o3 = o2 + MLA_ROPE
    o4 = o3 + SB_WIDTH
    o5 = o4 + SB_WIDTH
    w_kpe = w_in[:, o2:o3]
    cols = [w_in[:, :o2], _frame(w_kpe, MLA_ROPE, MLA_NOPE), _frame(_swap_halves(w_kpe, MLA_ROPE), MLA_ROPE, MLA_NOPE)]
    cols += [w_in[:, o3:]] if sample else [w_in[:, o4:o5]]
    win = jnp.concatenate(cols, axis=1).astype(BF16)
    dq = MLA_NOPE + MLA_ROPE
    wq3 = w_uq.reshape(Q_LORA, MLA_HEADS, dq)
    wa = _frame(w_uq, dq, 0)
    pe_sw = _swap_halves(wq3[..., MLA_NOPE:].reshape(Q_LORA, -1), MLA_ROPE)
    wb = _frame(pe_sw, MLA_ROPE, MLA_NOPE)
    return win, wa.astype(BF16), wb.astype(BF16)


def _proj_prompt(x2, seq_len, pos, g, w_in, q_norm, w_uq, kv_norm, w_ukv, tm):
    n, d = x2.shape
    win, wa, wb = _proj_weights(w_in, w_uq, sample=False)
    o_sq = Q_LORA + KV_LORA + MLA_ROPE
    wsq_t = w_in[:, o_sq:o_sq + SB_WIDTH].T.astype(BF16)
    wskv_t = w_in[:, o_sq + SB_WIDTH:].T.astype(BF16)
    wukv = w_ukv.astype(BF16)
    cos, sin = _rope_tables(pos)
    hw = MLA_HEADS * LANES
    tps = seq_len // tm
    nseq = n // seq_len
    row = lambda w: pl.BlockSpec((tm, w), lambda i: (i, 0))
    slab = pl.BlockSpec((None, None, hw, tm), lambda i: (i // tps, i % tps, 0, 0))
    col = pl.BlockSpec((None, SB_WIDTH, tm), lambda i: (i // tps, 0, i % tps))
    tcol = pl.BlockSpec((LANES, tm), lambda i: (0, i))
    slab_t = jax.ShapeDtypeStruct((nseq, tps, hw, tm), BF16)
    skv_t = jax.ShapeDtypeStruct((nseq, SB_WIDTH, seq_len), F32)
    outs = [slab_t, slab_t, jax.ShapeDtypeStruct((n, hw), BF16), jax.ShapeDtypeStruct((n, SB_WIDTH), BF16), slab_t,
            jax.ShapeDtypeStruct((n, KV_LORA), F32), jax.ShapeDtypeStruct((n, LANES), F32), skv_t, skv_t]
    return pl.pallas_call(
        _proj_prompt_kernel,
        grid=(n // tm,),
        in_specs=[row(d), _const_spec((1, d)), _const_spec(win.shape), _const_spec((1, Q_LORA)),
                  _const_spec((1, KV_LORA)), _const_spec(wukv.shape), _const_spec((hw, Q_LORA)),
                  _const_spec((hw, Q_LORA)), _const_spec(wsq_t.shape), _const_spec(wskv_t.shape),
                  _const_spec((hw, KV_LORA)), row(LANES), row(LANES), tcol, tcol],
        out_specs=[slab, slab, row(hw), row(SB_WIDTH), slab, row(KV_LORA), row(LANES), col, col],
        out_shape=outs,
        compiler_params=_cparams(("parallel",)),
        name="proj_prompt",
    )(x2, g.reshape(1, d), win, q_norm.reshape(1, -1), kv_norm.reshape(1, -1), wukv, wa.T, wb.T,
      wsq_t, wskv_t, wukv.T, cos, sin, cos.T, sin.T)


def _proj_sample(x2, pos, g, w_in, q_norm, w_uq, kv_norm, w_ukv, tm):
    n, d = x2.shape
    win, wa, wb = _proj_weights(w_in, w_uq, sample=True)
    cos, sin = _rope_tables(pos)
    hw = MLA_HEADS * LANES
    w3k = w_ukv.reshape(KV_LORA, MLA_HEADS, MLA_NOPE + MLA_V)[..., :MLA_NOPE]
    wabs = jnp.zeros((MLA_HEADS, LANES, MLA_HEADS, KV_LORA), F32)
    for h in range(MLA_HEADS):
        wabs = wabs.at[h, :MLA_NOPE, h, :].set(w3k[:, h, :].T)
    wabs = wabs.reshape(hw, MLA_HEADS * KV_LORA).astype(BF16)
    row = lambda w: pl.BlockSpec((tm, w), lambda i: (i, 0))
    outs = [jax.ShapeDtypeStruct((n, hw), F32), jax.ShapeDtypeStruct((n, MLA_HEADS * KV_LORA), F32),
            jax.ShapeDtypeStruct((n, SB_WIDTH), F32), jax.ShapeDtypeStruct((n, KV_LORA), F32),
            jax.ShapeDtypeStruct((n, LANES), F32), jax.ShapeDtypeStruct((n, SB_WIDTH), F32),
            jax.ShapeDtypeStruct((n, SB_WIDTH), F32)]
    return pl.pallas_call(
        _proj_sample_kernel,
        grid=(n // tm,),
        in_specs=[row(d), _const_spec((1, d)), _const_spec(win.shape), _const_spec((1, Q_LORA)),
                  _const_spec((1, KV_LORA)), _const_spec(wa.shape), _const_spec(wb.shape),
                  _const_spec(wabs.shape), row(LANES), row(LANES)],
        out_specs=[row(hw), row(MLA_HEADS * KV_LORA), row(SB_WIDTH), row(KV_LORA), row(LANES),
                   row(SB_WIDTH), row(SB_WIDTH)],
        out_shape=outs,
        compiler_params=_cparams(("parallel",)),
        name="proj_sample",
    )(x2, g.reshape(1, d), win, q_norm.reshape(1, -1), kv_norm.reshape(1, -1), wa, wb, wabs, cos, sin)


def _softmax_block(s, v, m, l, acc):
    m_new = jnp.maximum(m, jnp.max(s, axis=-1, keepdims=True))
    alpha = jnp.exp2(m - m_new)
    p = jnp.exp2(s - m_new)
    l = alpha * l + jnp.sum(p, axis=-1, keepdims=True)
    acc = alpha * acc + _dot(p.astype(BF16), v)
    return m_new, l, acc


def _stick_block(z, v, tri, c, acc, valid=None, v_transposed=False):
    e = jnp.exp2(-jnp.abs(z))
    lp = jnp.log(1.0 + e) * LOG2E
    mx = jnp.maximum(z, 0.0)
    ls = (z - mx) - lp
    nl = mx + lp
    if valid is not None:
        nl = jnp.where(valid, nl, 0.0)
    after = _dot(nl.astype(BF16), tri)
    a = jnp.exp2(ls - after - c)
    if valid is not None:
        a = jnp.where(valid, a, 0.0)
    a = a.astype(BF16)
    acc = acc + (_dot_nt(a, v) if v_transposed else _dot(a, v))
    c = c + jnp.sum(nl, axis=-1, keepdims=True)
    return c, acc


def _tri(k):
    r = lax.broadcasted_iota(jnp.int32, (k, k), 0)
    s = lax.broadcasted_iota(jnp.int32, (k, k), 1)
    return jnp.where(r > s, 1.0, 0.0).astype(BF16)


def _all_done(cs):
    cmin = cs[0]
    for c in cs[1:]:
        cmin = jnp.minimum(cmin, c)
    return (jnp.min(cmin) > SB_DONE).astype(jnp.int32)


def _softmax_block_t(st, vt, m, l, acc):
    m_new = jnp.maximum(m, jnp.max(st, axis=0, keepdims=True))
    alpha = jnp.exp2(m - m_new)
    p = jnp.exp2(st - m_new)
    l = alpha * l + jnp.sum(p, axis=0, keepdims=True)
    acc = alpha * acc + _dot(vt, p.astype(BF16))
    return m_new, l, acc


def _stick_block_t(zt, vt, tri_t, c, acc, valid=None):
    e = jnp.exp2(-jnp.abs(zt))
    lp = jnp.log(1.0 + e) * LOG2E
    mx = jnp.maximum(zt, 0.0)
    ls = (zt - mx) - lp
    nl = mx + lp
    if valid is not None:
        nl = jnp.where(valid, nl, 0.0)
    after = _dot(tri_t, nl.astype(BF16))
    a = jnp.exp2(ls - after - c)
    if valid is not None:
        a = jnp.where(valid, a, 0.0)
    acc = acc + _dot(vt, a.astype(BF16))
    c = c + jnp.sum(nl, axis=0, keepdims=True)
    return c, acc


def _attn_prompt_kernel(qmt_ref, qst_ref, km_ref, ks_ref, vvt_ref, o_ref, *, tq, kb, hg):
    i = pl.program_id(2)
    heads = range(hg)
    kl = [pl.ds((h // 2) * LANES, LANES) for h in heads]
    hs = [pl.ds(h * LANES, LANES) for h in heads]
    qmt = [qmt_ref[hs[h], :] for h in heads]
    qst = [qst_ref[hs[h], :] for h in heads]
    tk = kb * tq

    def keys(start, size):
        return pl.ds(pl.multiple_of(start, tq), size)

    def values(h, blk0, nblk):
        return jnp.concatenate([vvt_ref[blk0 + j, hs[h], :] for j in range(nblk)], axis=1)

    n_full = i // kb

    def scores(p):
        r = keys(p * tk, tk)
        return tuple(_dot(km_ref[r, hs[h]], qmt[h]) for h in heads)

    def mla_body(p, carry):
        s_cur, state = carry
        s_next = scores(p + 1)
        state = tuple(_softmax_block_t(s_cur[h], values(h, p * kb, kb), *state[h]) for h in heads)
        return s_next, state

    init = tuple((jnp.full((1, tq), -jnp.inf, F32), jnp.zeros((1, tq), F32), jnp.zeros((LANES, tq), F32))
                 for _ in heads)
    s_diag, state = lax.fori_loop(0, n_full, mla_body, (scores(0), init))
    kpos = n_full * tk + lax.broadcasted_iota(jnp.int32, (tk, tq), 0)
    qpos = i * tq + lax.broadcasted_iota(jnp.int32, (tk, tq), 1)
    causal = kpos <= qpos
    state = tuple(_softmax_block_t(jnp.where(causal, s_diag[h], -jnp.inf), values(h, n_full * kb, kb), *state[h])
                  for h in heads)

    j0 = jnp.maximum(i - 1, 0)
    krow2 = lax.broadcasted_iota(jnp.int32, (2 * tq, 2 * tq), 0)
    kcol2 = lax.broadcasted_iota(jnp.int32, (2 * tq, 2 * tq), 1)
    tri2_t = jnp.where(kcol2 > krow2, 1.0, 0.0).astype(BF16)
    kpos = j0 * tq + lax.broadcasted_iota(jnp.int32, (2 * tq, tq), 0)
    qpos = i * tq + lax.broadcasted_iota(jnp.int32, (2 * tq, tq), 1)
    sb = tuple(_stick_block_t(_dot(ks_ref[keys(j0 * tq, 2 * tq), kl[h]], qst[h]), values(h, j0, 2), tri2_t,
                              jnp.zeros((1, tq), F32), jnp.zeros((LANES, tq), F32), valid=kpos < qpos)
               for h in heads)

    def sb_cond(carry):
        j, done, _ = carry
        return jnp.logical_and(j >= 0, done == 0)

    def sb_body(carry):
        j, _, sb = carry
        krow = lax.broadcasted_iota(jnp.int32, (tq, tq), 0)
        kcol = lax.broadcasted_iota(jnp.int32, (tq, tq), 1)
        tri_t = jnp.where(kcol > krow, 1.0, 0.0).astype(BF16)
        sb = tuple(_stick_block_t(_dot(ks_ref[keys(j * tq, tq), kl[h]], qst[h]), values(h, j, 1), tri_t, *sb[h])
                   for h in heads)
        return j - 1, _all_done([st[0] for st in sb]), sb

    _, _, sb = lax.while_loop(sb_cond, sb_body, (j0 - 1, _all_done([st[0] for st in sb]), sb))

    top = lax.broadcasted_iota(jnp.int32, (LANES, tq), 0) < HALF
    for h in heads:
        _, l, acc_mla = state[h]
        o_t = jnp.where(top, sb[h][1], acc_mla / l)
        o_ref[:, hs[h]] = o_t.T.astype(o_ref.dtype)


def _attn_prompt(qmt, qst, km, ks, vvt, kb, hg):
    b, nblk, hw, tq = qmt.shape
    s = nblk * tq
    pair = hg * LANES
    assert hg % 2 == 0 and hw % pair == 0 and nblk % kb == 0 and 2 * SB_HEAD_DIM == LANES
    resident = dict(pipeline_mode=pl.Buffered(1))
    qspec = pl.BlockSpec((None, None, pair, tq), lambda bi, g, i: (bi, i, g, 0))
    return pl.pallas_call(
        functools.partial(_attn_prompt_kernel, tq=tq, kb=kb, hg=hg),
        grid=(b, hw // pair, nblk),
        in_specs=[qspec, qspec,
                  pl.BlockSpec((None, s, pair), lambda bi, g, i: (bi, 0, g), **resident),
                  pl.BlockSpec((None, s, pair // 2), lambda bi, g, i: (bi, 0, g), **resident),
                  pl.BlockSpec((None, nblk, pair, tq), lambda bi, g, i: (bi, 0, g, 0), **resident)],
        out_specs=pl.BlockSpec((None, tq, pair), lambda bi, g, i: (bi, i, g)),
        out_shape=jax.ShapeDtypeStruct((b, s, hw), BF16),
        compiler_params=_cparams(("parallel", "parallel", "arbitrary")),
        name="attn_prompt",
    )(qmt, qst, km, ks, vvt)


def _attn_sample_kernel(pt_ref, qf_ref, ql_ref, sq_ref, ckvn_ref, kpen_ref, skn_ref, svn_ref,
                        ckv_hbm, kpe_hbm, sbk_hbm, sbv_hbm, olat_ref, osb_ref,
                        ckv_buf, kpe_buf, sbk_buf, sbv_buf, ckv_sem, kpe_sem, sbk_sem, sbv_sem,
                        *, t_new, layer, n_pages, chunk_pages):
    b = pl.program_id(0)
    nb = pl.num_programs(0)
    nh = MLA_HEADS
    rows = nh * t_new
    page = PAGE_SIZE
    n_chunks = n_pages // chunk_pages
    n_pairs = n_chunks // 2

    def ckv_copy(pg, slot, p):
        return pltpu.make_async_copy(ckv_hbm.at[layer, pg], ckv_buf.at[slot, pl.ds(p * page, page), :],
                                     ckv_sem.at[slot])

    def kpe_copy(pg, slot, p):
        return pltpu.make_async_copy(kpe_hbm.at[layer, pg], kpe_buf.at[slot, :, pl.ds(p * page, page)],
                                     kpe_sem.at[slot])

    def sbk_copy(pg, slot):
        return pltpu.make_async_copy(sbk_hbm.at[layer, pg], sbk_buf.at[slot], sbk_sem.at[slot])

    def sbv_copy(pg, slot):
        return pltpu.make_async_copy(sbv_hbm.at[layer, pg], sbv_buf.at[slot], sbv_sem.at[slot])

    def start_chunk(seq, c, slot):
        for p in range(chunk_pages):
            pg = pt_ref[seq, c * chunk_pages + p]
            ckv_copy(pg, slot, p).start()
            kpe_copy(pg, slot, p).start()

    def wait_chunk(slot):
        for p in range(chunk_pages):
            ckv_copy(0, slot, p).wait()
            kpe_copy(0, slot, p).wait()

    def start_sb(pg, slot):
        sbk_copy(pg, slot).start()
        sbv_copy(pg, slot).start()

    def wait_sb(slot):
        sbk_copy(0, slot).wait()
        sbv_copy(0, slot).wait()

    @pl.when(b == 0)
    def _():
        start_chunk(0, 0, 0)

    start_sb(pt_ref[b, n_pages - 1], 0)
    start_sb(pt_ref[b, n_pages - 2], 1)

    qf = qf_ref[...]
    qlr = ql_ref[...]
    sq = sq_ref[...]
    ql = jnp.concatenate([qlr[:, h * KV_LORA:(h + 1) * KV_LORA] for h in range(nh)], axis=0).astype(BF16)
    qp = jnp.concatenate(
        [qf[:, h * LANES + MLA_NOPE:h * LANES + MLA_NOPE + MLA_ROPE] for h in range(nh)], axis=0).astype(BF16)
    lane_head = lax.broadcasted_iota(jnp.int32, (rows, SB_WIDTH), 1) // SB_HEAD_DIM
    row_head = lax.broadcasted_iota(jnp.int32, (rows, SB_WIDTH), 0) // t_new
    qbd = jnp.where(lane_head == row_head, jnp.tile(sq, (nh, 1)), 0.0).astype(BF16)

    def pad(x):
        return jnp.concatenate([x, jnp.zeros((page - t_new, x.shape[1]), x.dtype)], axis=0).astype(BF16)

    qidx = lax.broadcasted_iota(jnp.int32, (rows, page), 0) % t_new
    kidx = lax.broadcasted_iota(jnp.int32, (rows, page), 1)
    ckv_n = pad(ckvn_ref[...])
    s = _dot_nt(ql, ckv_n) + _dot_nt(qp, pad(kpen_ref[...]))
    s = jnp.where(kidx <= qidx, s, -jnp.inf)
    mla = _softmax_block(s, ckv_n, jnp.full((rows, 1), -jnp.inf, F32), jnp.zeros((rows, 1), F32),
                         jnp.zeros((rows, KV_LORA), F32))

    def mla_chunk(slot, mla):
        ckv = ckv_buf[slot].astype(BF16)
        s = _dot_nt(ql, ckv) + _dot(qp, kpe_buf[slot].astype(BF16))
        return _softmax_block(s, ckv, *mla)

    tri = _tri(page)

    def sb_page(slot, sb):
        return _stick_block(_dot(qbd, sbk_buf[slot].astype(BF16)), sbv_buf[slot].astype(BF16), tri, *sb,
                            v_transposed=True)

    def sb_first():
        sb = _stick_block(_dot_nt(qbd, pad(skn_ref[...])), pad(svn_ref[...]), tri, jnp.zeros((rows, 1), F32),
                          jnp.zeros((rows, SB_WIDTH), F32), valid=kidx < qidx)
        return sb_page(1, sb_page(0, sb))

    def chunk_pair(t, mla, with_sb):
        c0 = 2 * t
        start_chunk(b, c0 + 1, 1)
        wait_chunk(0)
        mla = mla_chunk(0, mla)

        @pl.when(c0 + 2 < n_chunks)
        def _():
            start_chunk(b, c0 + 2, 0)

        @pl.when(jnp.logical_and(c0 + 2 >= n_chunks, b + 1 < nb))
        def _():
            start_chunk(b + 1, 0, 0)

        wait_chunk(1)
        if not with_sb:
            return mla_chunk(1, mla)
        wait_sb(0)
        wait_sb(1)
        return mla_chunk(1, mla), sb_first()

    mla, sb = chunk_pair(jnp.int32(0), mla, True)
    _, l, acc = lax.fori_loop(1, n_pairs, lambda t, mla: chunk_pair(t, mla, False), mla)
    olat = acc / l
    for h in range(nh):
        olat_ref[:, h * KV_LORA:(h + 1) * KV_LORA] = olat[h * t_new:(h + 1) * t_new, :]


    def sb_cond(carry):
        j, done, _ = carry
        return jnp.logical_and(j >= 0, done == 0)

    def sb_body(carry):
        j, _, sb = carry
        start_sb(pt_ref[b, j], 0)
        wait_sb(0)
        sb = sb_page(0, sb)
        return j - 1, _all_done([sb[0]]), sb

    _, _, sb = lax.while_loop(sb_cond, sb_body, (jnp.int32(n_pages - 3), _all_done([sb[0]]), sb))
    a_sb = sb[1]
    head_of_lane = lax.broadcasted_iota(jnp.int32, (t_new, SB_WIDTH), 1) // SB_HEAD_DIM
    o = jnp.zeros((t_new, SB_WIDTH), F32)
    for h in range(nh):
        o = o + jnp.where(head_of_lane == h, a_sb[h * t_new:(h + 1) * t_new, :], 0.0)
    osb_ref[...] = o


def _attn_sample(page_table, qf, ql, sq, ckv_new, kpe_new, sk_new, sv_new,
                 cache_ckv, cache_kpe_t, cache_sbk_t, cache_sbv_t, layer, t_new):
    nb, n_pages = page_table.shape
    chunk_pages = min(MLA_CHUNK_PAGES, n_pages // 2)
    assert n_pages >= 2 and n_pages % (2 * chunk_pages) == 0
    seq = lambda w: pl.BlockSpec((t_new, w), lambda b, pt: (b, 0))
    hbm = pl.BlockSpec(memory_space=pl.ANY)
    ck = chunk_pages * PAGE_SIZE
    grid_spec = pltpu.PrefetchScalarGridSpec(
        num_scalar_prefetch=1,
        grid=(nb,),
        in_specs=[seq(MLA_HEADS * LANES), seq(MLA_HEADS * KV_LORA), seq(SB_WIDTH), seq(KV_LORA), seq(MLA_ROPE),
                  seq(SB_WIDTH), seq(SB_WIDTH), hbm, hbm, hbm, hbm],
        out_specs=[seq(MLA_HEADS * KV_LORA), seq(SB_WIDTH)],
        scratch_shapes=[pltpu.VMEM((2, ck, KV_LORA), F32), pltpu.VMEM((2, MLA_ROPE, ck), F32),
                        pltpu.VMEM((2, SB_WIDTH, PAGE_SIZE), F32), pltpu.VMEM((2, SB_WIDTH, PAGE_SIZE), F32),
                        pltpu.SemaphoreType.DMA((2,)), pltpu.SemaphoreType.DMA((2,)),
                        pltpu.SemaphoreType.DMA((2,)), pltpu.SemaphoreType.DMA((2,))],
    )
    n = nb * t_new
    return pl.pallas_call(
        functools.partial(_attn_sample_kernel, t_new=t_new, layer=layer, n_pages=n_pages,
                          chunk_pages=chunk_pages),
        grid_spec=grid_spec,
        out_shape=[jax.ShapeDtypeStruct((n, MLA_HEADS * KV_LORA), F32), jax.ShapeDtypeStruct((n, SB_WIDTH), F32)],
        compiler_params=_cparams(("arbitrary",)),
        name="attn_sample",
    )(page_table, qf, ql, sq, ckv_new, kpe_new, sk_new, sv_new, cache_ckv, cache_kpe_t, cache_sbk_t, cache_sbv_t)


def _ffn_tail(x, mix, g_post, g_fpre, g_fpost, wg_ref, wu_ref, wd_ref):
    y = x + _rms(mix, g_post)
    h = _rms(y, g_fpre).astype(BF16)
    gate = _dot(h, wg_ref[...])
    up = _dot(h, wu_ref[...])
    act = (gate * jax.nn.sigmoid(gate) * up).astype(BF16)
    return y + _rms(_dot(act, wd_ref[...]), g_fpost)


def _post_prompt_kernel(x_ref, a_ref, wo_ref, gp_ref, gf_ref, gq_ref, wg_ref, wu_ref, wd_ref, y_ref):
    mix = _dot(a_ref[...], wo_ref[...])
    y_ref[...] = _ffn_tail(x_ref[...], mix, gp_ref[...], gf_ref[...], gq_ref[...], wg_ref, wu_ref, wd_ref)


def _post_sample_kernel(x_ref, olat_ref, osb_ref, wv_ref, wom_ref, wos_ref, gp_ref, gf_ref, gq_ref,
                        wg_ref, wu_ref, wd_ref, y_ref):
    o_mla = _dot(olat_ref[...].astype(BF16), wv_ref[...])
    mix = _dot(o_mla.astype(BF16), wom_ref[...]) + _dot(osb_ref[...].astype(BF16), wos_ref[...])
    y_ref[...] = _ffn_tail(x_ref[...], mix, gp_ref[...], gf_ref[...], gq_ref[...], wg_ref, wu_ref, wd_ref)


def _ffn_specs(d, dff):
    return [_const_spec((1, d)), _const_spec((1, d)), _const_spec((1, d)),
            _const_spec((d, dff)), _const_spec((d, dff)), _const_spec((dff, d))]


def _post_prompt(x2, a2, w_out, g_post, g_fpre, g_fpost, wg, wu, wd, tm):
    n, d = x2.shape
    wm = w_out[:MLA_HEADS * MLA_V].reshape(MLA_HEADS, MLA_V, d)
    ws = w_out[MLA_HEADS * MLA_V:].reshape(SB_HEADS, SB_HEAD_DIM, d)
    wo = jnp.concatenate([ws, wm], axis=1).reshape(-1, d).astype(BF16)
    row = lambda w: pl.BlockSpec((tm, w), lambda i: (i, 0))
    return pl.pallas_call(
        _post_prompt_kernel,
        grid=(n // tm,),
        in_specs=[row(d), row(a2.shape[1]), _const_spec(wo.shape)] + _ffn_specs(d, wg.shape[1]),
        out_specs=row(d),
        out_shape=jax.ShapeDtypeStruct((n, d), F32),
        compiler_params=_cparams(("parallel",)),
        name="post_prompt",
    )(x2, a2, wo, g_post.reshape(1, d), g_fpre.reshape(1, d), g_fpost.reshape(1, d), wg, wu, wd)


def _post_sample(x2, olat, osb, w_ukv, w_out, g_post, g_fpre, g_fpost, wg, wu, wd, tm):
    n, d = x2.shape
    w3v = w_ukv.reshape(KV_LORA, MLA_HEADS, MLA_NOPE + MLA_V)[..., MLA_NOPE:]
    wv = jnp.zeros((MLA_HEADS, KV_LORA, MLA_HEADS, MLA_V), F32)
    for h in range(MLA_HEADS):
        wv = wv.at[h, :, h, :].set(w3v[:, h, :])
    wv = wv.reshape(MLA_HEADS * KV_LORA, MLA_HEADS * MLA_V).astype(BF16)
    wom = w_out[:MLA_HEADS * MLA_V].astype(BF16)
    wos = w_out[MLA_HEADS * MLA_V:].astype(BF16)
    row = lambda w: pl.BlockSpec((tm, w), lambda i: (i, 0))
    return pl.pallas_call(
        _post_sample_kernel,
        grid=(n // tm,),
        in_specs=[row(d), row(olat.shape[1]), row(osb.shape[1]), _const_spec(wv.shape), _const_spec(wom.shape),
                  _const_spec(wos.shape)] + _ffn_specs(d, wg.shape[1]),
        out_specs=row(d),
        out_shape=jax.ShapeDtypeStruct((n, d), F32),
        compiler_params=_cparams(("parallel",)),
        name="post_sample",
    )(x2, olat, osb, wv, wom, wos, g_post.reshape(1, d), g_fpre.reshape(1, d), g_fpost.reshape(1, d), wg, wu, wd)


HALO = 16


def _pool_prompt_kernel(x_ref, xh_ref, gm_ref, wp_ref, ps_ref, gp_ref, gf_ref, gq_ref, wg_ref, wu_ref, wd_ref,
                        y_ref, hist_ref, ext_s, *, tm, tiles_per_seq):
    i = pl.program_id(0)
    t_in_seq = i % tiles_per_seq
    x = x_ref[...]
    gm = gm_ref[...]
    hp = _rms(x, gm)
    halo = _rms(xh_ref[...], gm)
    halo = jnp.where(t_in_seq == 0, 0.0, halo)
    ext_s[0:HALO, :] = halo
    ext_s[HALO:HALO + tm, :] = hp
    hist_ref[...] = hp[tm - HALO:, :]
    grp = x.shape[1] // len(POOL_WINDOWS)
    pos = t_in_seq * tm + lax.broadcasted_iota(jnp.int32, (tm, 1), 0)
    outs = []
    for g, w in enumerate(POOL_WINDOWS):
        cs = slice(g * grp, (g + 1) * grp)
        cur = ext_s[HALO:HALO + tm, cs]
        tot = cur
        for k in range(1, w):
            tot = tot + ext_s[HALO - k:HALO - k + tm, cs]
        count = jnp.minimum(pos + 1, w).astype(F32)
        dlt = (tot / count - cur).astype(BF16)
        outs.append(_dot(dlt, wp_ref[g]))
    mix = jnp.concatenate(outs, axis=-1) * ps_ref[...]
    y_ref[...] = _ffn_tail(x, mix, gp_ref[...], gf_ref[...], gq_ref[...], wg_ref, wu_ref, wd_ref)


def _pool_prompt(x2, seq_len, g_mix, w_pool, pool_scale, g_post, g_fpre, g_fpost, wg, wu, wd, tm):
    n, d = x2.shape
    tiles_per_seq = seq_len // tm
    nseq = n // seq_len
    hb = tm // HALO
    row = pl.BlockSpec((tm, d), lambda i: (i, 0))
    halo = pl.BlockSpec((HALO, d), lambda i: (jnp.maximum(i * hb - 1, 0), 0))
    hist = pl.BlockSpec((None, HALO, d), lambda i: (i // tiles_per_seq, 0, 0))
    return pl.pallas_call(
        functools.partial(_pool_prompt_kernel, tm=tm, tiles_per_seq=tiles_per_seq),
        grid=(n // tm,),
        in_specs=[row, halo, _const_spec((1, d)), _const_spec(w_pool.shape), _const_spec((1, d))]
        + _ffn_specs(d, wg.shape[1]),
        out_specs=[row, hist],
        out_shape=[jax.ShapeDtypeStruct((n, d), F32), jax.ShapeDtypeStruct((nseq, HALO, d), F32)],
        scratch_shapes=[pltpu.VMEM((HALO + tm, d), F32)],
        compiler_params=_cparams(("arbitrary",)),
        name="pool_prompt",
    )(x2, x2, g_mix.reshape(1, d), w_pool.astype(BF16), pool_scale.reshape(1, d),
      g_post.reshape(1, d), g_fpre.reshape(1, d), g_fpost.reshape(1, d), wg, wu, wd)


def _pool_sample_kernel(x_ref, st_ref, gm_ref, wp_ref, ps_ref, gp_ref, gf_ref, gq_ref, wg_ref, wu_ref, wd_ref,
                        y_ref, hs_ref, *, t_new):
    nb, d = x_ref.shape[1], x_ref.shape[2]
    gm = gm_ref[...]
    x = x_ref[...]
    hs = _rms(x, gm)
    hs_ref[...] = hs
    ext = [st_ref[k] for k in range(POOL_HIST)] + [hs[t] for t in range(t_new)]
    grp = d // len(POOL_WINDOWS)
    outs = []
    for g, w in enumerate(POOL_WINDOWS):
        cs = slice(g * grp, (g + 1) * grp)
        dl = []
        for t in range(t_new):
            cur = ext[POOL_HIST + t][:, cs]
            tot = cur
            for k in range(1, w):
                tot = tot + ext[POOL_HIST + t - k][:, cs]
            dl.append(tot / float(w) - cur)
        dlt = jnp.concatenate(dl, axis=0).astype(BF16)
        outs.append(_dot(dlt, wp_ref[g]))
    mix = jnp.concatenate(outs, axis=-1) * ps_ref[...]
    y = _ffn_tail(x.reshape(t_new * nb, d), mix, gp_ref[...], gf_ref[...], gq_ref[...], wg_ref, wu_ref, wd_ref)
    y_ref[...] = y.reshape(t_new, nb, d)


def _pool_sample(x_t, st_t, g_mix, w_pool, pool_scale, g_post, g_fpre, g_fpost, wg, wu, wd, nb):
    t_new, b, d = x_t.shape
    slab = lambda r: pl.BlockSpec((r, nb, d), lambda i: (0, i, 0))
    return pl.pallas_call(
        functools.partial(_pool_sample_kernel, t_new=t_new),
        grid=(b // nb,),
        in_specs=[slab(t_new), slab(POOL_HIST), _const_spec((1, d)), _const_spec(w_pool.shape),
                  _const_spec((1, d))] + _ffn_specs(d, wg.shape[1]),
        out_specs=[slab(t_new), slab(t_new)],
        out_shape=[jax.ShapeDtypeStruct((t_new, b, d), F32), jax.ShapeDtypeStruct((t_new, b, d), F32)],
        compiler_params=_cparams(("parallel",)),
        name="pool_sample",
    )(x_t, st_t, g_mix.reshape(1, d), w_pool.astype(BF16), pool_scale.reshape(1, d),
      g_post.reshape(1, d), g_fpre.reshape(1, d), g_fpost.reshape(1, d), wg, wu, wd)


def kernel(x_prompt, x_sample, cache_ckv, cache_kpe, cache_sb_k, cache_sb_v, state_pool, page_table,
           ln_mix_pre, ln_mix_post, ln_ffn_pre, ln_ffn_post, w_in, q_norm, w_uq, kv_norm, w_ukv, w_out,
           w_pool, pool_scale, w_gate, w_up, w_down):
    bsz, seq, d = x_prompt.shape
    nb, t_new, _ = x_sample.shape
    n_pages = page_table.shape[1]
    past = n_pages * PAGE_SIZE
    n_p, n_s = bsz * seq, nb * t_new
    tm_p = min(512, seq)
    tm_s = min(512, n_s)
    tq = min(256, seq // 2)

    wg = w_gate.astype(BF16)
    wu = w_up.astype(BF16)
    wd = w_down.astype(BF16)
    xp = x_prompt.reshape(n_p, d)
    xs = x_sample.reshape(n_s, d)
    pos_p = jnp.tile(jnp.arange(seq, dtype=F32), bsz)
    pos_s = jnp.tile(past + jnp.arange(t_new, dtype=F32), nb)

    lw = (ln_mix_pre[0], w_in[0], q_norm[0], w_uq[0], kv_norm[0], w_ukv[0])
    qmt, qst, km, ks, vvt, ckv_p, kpe_p, skt_p, svt_p = _proj_prompt(xp, seq, pos_p, *lw, tm=tq)
    o_p = _attn_prompt(qmt, qst, km.reshape(bsz, seq, -1), ks.reshape(bsz, seq, -1), vvt, kb=2, hg=4)
    tail0 = (ln_mix_post[0], ln_ffn_pre[0], ln_ffn_post[0], wg[0], wu[0], wd[0])
    yp = _post_prompt(xp, o_p.reshape(n_p, -1), w_out[0], *tail0, tm=tm_p)

    qf, ql, sq, ckv_s, kpe_s, sk_s, sv_s = _proj_sample(xs, pos_s, *lw, tm=min(256, n_s))
    kpe_s32 = kpe_s[:, MLA_NOPE:MLA_NOPE + MLA_ROPE]
    n_phys = cache_ckv.shape[1]
    nl = cache_sb_k.shape[0]
    kpe_t = jnp.swapaxes(cache_kpe, 2, 3)
    sbk_t = jnp.transpose(cache_sb_k, (0, 1, 3, 4, 2)).reshape(nl, n_phys, SB_WIDTH, PAGE_SIZE)
    sbv_t = jnp.transpose(cache_sb_v, (0, 1, 3, 4, 2)).reshape(nl, n_phys, SB_WIDTH, PAGE_SIZE)
    olat, osb = _attn_sample(page_table, qf, ql, sq, ckv_s, kpe_s32, sk_s, sv_s,
                             cache_ckv, kpe_t, sbk_t, sbv_t, 0, t_new)
    ys = _post_sample(xs, olat, osb, w_ukv[0], w_out[0], *tail0, tm=tm_s)

    tail1 = (ln_mix_post[1], ln_ffn_pre[1], ln_ffn_post[1], wg[1], wu[1], wd[1])
    yp, hist_p = _pool_prompt(yp, seq, ln_mix_pre[1], w_pool[0], pool_scale[0], *tail1, tm=tm_p)
    ys_t = jnp.swapaxes(ys.reshape(nb, t_new, d), 0, 1)
    st_t = jnp.swapaxes(state_pool[0], 0, 1)
    ys_t, hs_t = _pool_sample(ys_t, st_t, ln_mix_pre[1], w_pool[0], pool_scale[0], *tail1, nb=min(64, nb))
    ys = jnp.swapaxes(ys_t, 0, 1)
    hs = jnp.swapaxes(hs_t, 0, 1)

    new_pool_s = jnp.concatenate([state_pool[0], hs], axis=1)[:, t_new:]
    return (yp.reshape(bsz, seq, d), ys,
            ckv_p.reshape(1, bsz, seq, KV_LORA),
            kpe_p[:, MLA_NOPE:MLA_NOPE + MLA_ROPE].reshape(1, bsz, seq, MLA_ROPE),
            jnp.transpose(skt_p.reshape(1, bsz, SB_HEADS, SB_HEAD_DIM, seq), (0, 1, 4, 2, 3)),
            jnp.transpose(svt_p.reshape(1, bsz, SB_HEADS, SB_HEAD_DIM, seq), (0, 1, 4, 2, 3)),
            hist_p[:, HALO - POOL_HIST:][None],
            ckv_s.reshape(1, nb, t_new, KV_LORA),
            kpe_s32.reshape(1, nb, t_new, MLA_ROPE),
            sk_s.reshape(1, nb, t_new, SB_HEADS, SB_HEAD_DIM),
            sv_s.reshape(1, nb, t_new, SB_HEADS, SB_HEAD_DIM),
            new_pool_s[None])
```
